```python
import math
import jax, jax.numpy as jnp
from jax import lax
import numpy as np

D_MODEL = 2048
BATCH = 8
SEQ = 2048
DEPTH = 2
DEC_BATCH = 128
DEC_SEQ = 4
PAST_LEN = 2048
PAGE_SIZE = 128

N_A_LAYERS = DEPTH // 2
N_B_LAYERS = DEPTH - N_A_LAYERS
RET_HEADS = 8
RET_DK = D_MODEL // RET_HEADS
RET_DV = 2 * RET_DK
RET_CHUNK = 128
SB_HEADS = 16
SB_HD = D_MODEL // SB_HEADS
SB_BLOCK = 128
SB_BIAS_INIT = -6.0
D_FF = ((8 * D_MODEL // 3 + 255) // 256) * 256
ROPE_BASE = 10000.0
EPS = 1e-6
FFN_RES = 0.5
RET_Q_END = RET_HEADS * RET_DK
RET_K_END = 2 * RET_HEADS * RET_DK
RET_V_END = RET_K_END + RET_HEADS * RET_DV
RET_IN = RET_V_END + RET_HEADS * RET_DV

kernel_name = "yoco_retention_stickbreaking_macaron_step"


def rmsnorm(x, g):
    xf = x.astype(jnp.float32)
    y = xf * lax.rsqrt(jnp.mean(xf * xf, axis=-1, keepdims=True) + EPS)
    return (y * g.astype(jnp.float32)).astype(x.dtype)


def head_rmsnorm(x, g):
    xf = x.astype(jnp.float32)
    return xf * lax.rsqrt(jnp.mean(xf * xf, axis=-1, keepdims=True) + EPS) * g.astype(jnp.float32)


def swiglu(h, w_in, w_out):
    gate, up = jnp.split(h @ w_in, 2, axis=-1)
    return (jax.nn.silu(gate) * up) @ w_out


def rope(x, pos):
    half = x.shape[-1] // 2
    inv = ROPE_BASE ** (-jnp.arange(half, dtype=jnp.float32) / half)
    ang = pos[:, None] * inv[None, :]
    cos = jnp.cos(ang)[None, :, None, :]
    sin = jnp.sin(ang)[None, :, None, :]
    x1, x2 = x[..., :half], x[..., half:]
    return jnp.concatenate([x1 * cos - x2 * sin, x1 * sin + x2 * cos], axis=-1)


def retention_log_gamma():
    return jnp.log(1.0 - 2.0 ** (-5.0 - jnp.arange(RET_HEADS, dtype=jnp.float32)))


def retention_chunk(q, k, v, state, log_gamma):
    c = q.shape[1]
    idx = jnp.arange(c, dtype=jnp.float32)
    diff = idx[:, None] - idx[None, :]
    decay = jnp.where(diff[None] >= 0,
                      jnp.exp(log_gamma[:, None, None] * jnp.maximum(diff, 0.0)[None]), 0.0)
    scores = jnp.einsum('bihd,bjhd->bhij', q, k) * decay[None]
    inner = jnp.einsum('bhij,bjhe->bihe', scores, v)
    q_decay = jnp.exp((idx[:, None] + 1.0) * log_gamma[None, :])
    cross = jnp.einsum('bihd,bhde->bihe', q * q_decay[None, :, :, None], state)
    k_decay = jnp.exp((c - 1.0 - idx)[:, None] * log_gamma[None, :])
    new_state = (jnp.exp(c * log_gamma)[None, :, None, None] * state
                 + jnp.einsum('bjhd,bjhe->bhde', k * k_decay[None, :, :, None], v))
    return inner + cross, new_state


def retention_project(h, pos, w_in):
    b, t, _ = h.shape
    proj = h @ w_in
    q = proj[..., :RET_Q_END].reshape(b, t, RET_HEADS, RET_DK).astype(jnp.float32)
    k = proj[..., RET_Q_END:RET_K_END].reshape(b, t, RET_HEADS, RET_DK).astype(jnp.float32)
    v = proj[..., RET_K_END:RET_V_END].reshape(b, t, RET_HEADS, RET_DV).astype(jnp.float32)
    g = proj[..., RET_V_END:]
    return rope(q, pos), rope(k, pos) * (RET_DK ** -0.5), v, g


def retention_output(o, g, gn, w_out, dtype):
    b, t = o.shape[:2]
    o = head_rmsnorm(o, gn).reshape(b, t, RET_HEADS * RET_DV).astype(dtype)
    return (o * jax.nn.silu(g)) @ w_out


def retention_prompt(h, pos, w_in, gn, w_out, log_gamma):
    q, k, v, g = retention_project(h, pos, w_in)
    b, t = h.shape[:2]
    nc = t // RET_CHUNK

    def to_chunks(a):
        return a.reshape(b, nc, RET_CHUNK, *a.shape[2:]).swapaxes(0, 1)

    def step(state, qkv):
        qc, kc, vc = qkv
        o, st = retention_chunk(qc, kc, vc, state, log_gamma)
        return st, o

    state0 = jnp.zeros((b, RET_HEADS, RET_DK, RET_DV), jnp.float32)
    state, o = lax.scan(step, state0, (to_chunks(q), to_chunks(k), to_chunks(v)))
    o = o.swapaxes(0, 1).reshape(b, t, RET_HEADS, RET_DV)
    return retention_output(o, g, gn, w_out, h.dtype), state


def retention_sample(h, pos, state, w_in, gn, w_out, log_gamma):
    q, k, v, g = retention_project(h, pos, w_in)
    o, new_state = retention_chunk(q, k, v, state.astype(jnp.float32), log_gamma)
    return retention_output(o, g, gn, w_out, h.dtype), new_state


def stick_breaking(q, k, v, q_pos, k_pos, bias):
    z = jnp.einsum('bqhd,bkhd->bhqk', q, k,
                   preferred_element_type=jnp.float32) * (SB_HD ** -0.5)
    z = z + bias.astype(jnp.float32)[None, :, None, None]
    mask = (k_pos[None, :] < q_pos[:, None])[None, None]
    log_beta = jax.nn.log_sigmoid(z)
    log_keep = jnp.where(mask, jax.nn.log_sigmoid(-z), 0.0)
    tail = lax.cumsum(log_keep, axis=3, reverse=True) - log_keep
    a = jnp.where(mask, jnp.exp(log_beta + tail), 0.0)
    return jnp.einsum('bhqk,bkhd->bqhd', a.astype(v.dtype), v)


def sb_prompt(q, k, v, bias):
    b, s = q.shape[:2]
    nb = s // SB_BLOCK
    qb = q.reshape(b, nb, SB_BLOCK, SB_HEADS, SB_HD).swapaxes(0, 1)
    qpos = jnp.arange(s, dtype=jnp.int32).reshape(nb, SB_BLOCK)
    kpos = jnp.arange(s, dtype=jnp.int32)
    out = lax.map(lambda args: stick_breaking(args[0], k, v, args[1], kpos, bias), (qb, qpos))
    return out.swapaxes(0, 1).reshape(b, s, SB_HEADS, SB_HD)


def shared_kv(x, kv_norm, w_kv, k_norm):
    b, t, _ = x.shape
    kv = rmsnorm(x, kv_norm) @ w_kv
    k = kv[..., :SB_HEADS * SB_HD].reshape(b, t, SB_HEADS, SB_HD)
    v = kv[..., SB_HEADS * SB_HD:].reshape(b, t, SB_HEADS, SB_HD)
    return head_rmsnorm(k, k_norm).astype(x.dtype), v


def sb_query(h, w_q, q_norm):
    b, t, _ = h.shape
    q = (h @ w_q).reshape(b, t, SB_HEADS, SB_HD)
    return head_rmsnorm(q, q_norm).astype(h.dtype)


def setup_inputs(seed: int = 0) -> dict:
    key = jax.random.key(seed)
    ks = jax.random.split(key, 24)
    n_pages = PAST_LEN // PAGE_SIZE
    n_pool = (DEC_BATCH * n_pages * 5) // 4
    f32 = jnp.float32

    def w(k, shape, fan_in):
        return jax.random.normal(k, shape, f32) * (fan_in ** -0.5)

    def gain(k, shape):
        return 1.0 + 0.02 * jax.random.normal(k, shape, f32)

    page_table = jax.random.permutation(ks[5], n_pool)[:DEC_BATCH * n_pages]
    page_table = page_table.reshape(DEC_BATCH, n_pages).astype(jnp.int32)
    return {
        "x_prompt": jax.random.normal(ks[0], (BATCH, SEQ, D_MODEL), f32),
        "x_sample": jax.random.normal(ks[1], (DEC_BATCH, DEC_SEQ, D_MODEL), f32),
        "state_ret": jax.random.normal(ks[2], (N_A_LAYERS, DEC_BATCH, RET_HEADS, RET_DK, RET_DV), f32),
        "cache_k": jax.random.normal(ks[3], (n_pool, PAGE_SIZE, SB_HEADS, SB_HD), f32),
        "cache_v": jax.random.normal(ks[4], (n_pool, PAGE_SIZE, SB_HEADS, SB_HD), f32),
        "page_table": page_table,
        "ffn1_norm": gain(ks[6], (DEPTH, D_MODEL)),
        "ffn1_w_in": w(ks[7], (DEPTH, D_MODEL, 2 * D_FF), D_MODEL),
        "ffn1_w_out": w(ks[8], (DEPTH, D_FF, D_MODEL), D_FF),
        "mix_norm": gain(ks[9], (DEPTH, D_MODEL)),
        "ffn2_norm": gain(ks[10], (DEPTH, D_MODEL)),
        "ffn2_w_in": w(ks[11], (DEPTH, D_MODEL, 2 * D_FF), D_MODEL),
        "ffn2_w_out": w(ks[12], (DEPTH, D_FF, D_MODEL), D_FF),
        "ret_w_in": w(ks[13], (N_A_LAYERS, D_MODEL, RET_IN), D_MODEL),
        "ret_gn": gain(ks[14], (N_A_LAYERS, RET_HEADS, RET_DV)),
        "ret_w_out": w(ks[15], (N_A_LAYERS, RET_HEADS * RET_DV, D_MODEL), RET_HEADS * RET_DV),
        "kv_norm": gain(ks[16], (D_MODEL,)),
        "w_kv": w(ks[17], (D_MODEL, 2 * SB_HEADS * SB_HD), D_MODEL),
        "k_norm": gain(ks[18], (SB_HEADS, SB_HD)),
        "sb_w_q": w(ks[19], (N_B_LAYERS, D_MODEL, SB_HEADS * SB_HD), D_MODEL),
        "sb_q_norm": gain(ks[20], (N_B_LAYERS, SB_HEADS, SB_HD)),
        "sb_bias": SB_BIAS_INIT + 0.1 * jax.random.normal(ks[22], (N_B_LAYERS, SB_HEADS), f32),
        "sb_w_out": w(ks[21], (N_B_LAYERS, SB_HEADS * SB_HD, D_MODEL), SB_HEADS * SB_HD),
    }


def reference(x_prompt, x_sample, state_ret, cache_k, cache_v, page_table,
              ffn1_norm, ffn1_w_in, ffn1_w_out, mix_norm, ffn2_norm, ffn2_w_in, ffn2_w_out,
              ret_w_in, ret_gn, ret_w_out, kv_norm, w_kv, k_norm,
              sb_w_q, sb_q_norm, sb_bias, sb_w_out):
    seq = x_prompt.shape[1]
    dec_b, dec_s = x_sample.shape[:2]
    pos_p = jnp.arange(seq, dtype=jnp.float32)
    pos_s = PAST_LEN + jnp.arange(dec_s, dtype=jnp.float32)
    log_gamma = retention_log_gamma()
    xp, xs = x_prompt, x_sample
    ret_p, ret_s = [], []
    k_prompt = v_prompt = k_sample = v_sample = None
    k_all = v_all = kpos_s = None
    qpos_s = PAST_LEN + jnp.arange(dec_s, dtype=jnp.int32)

    for layer in range(DEPTH):
        if layer == N_A_LAYERS:
            k_prompt, v_prompt = shared_kv(xp, kv_norm, w_kv, k_norm)
            k_sample, v_sample = shared_kv(xs, kv_norm, w_kv, k_norm)
            past_k = cache_k[page_table].reshape(dec_b, -1, SB_HEADS, SB_HD).astype(xs.dtype)
            past_v = cache_v[page_table].reshape(dec_b, -1, SB_HEADS, SB_HD).astype(xs.dtype)
            k_all = jnp.concatenate([past_k, k_sample], axis=1)
            v_all = jnp.concatenate([past_v, v_sample], axis=1)
            kpos_s = jnp.arange(k_all.shape[1], dtype=jnp.int32)

        xp = xp + FFN_RES * swiglu(rmsnorm(xp, ffn1_norm[layer]), ffn1_w_in[layer], ffn1_w_out[layer])
        xs = xs + FFN_RES * swiglu(rmsnorm(xs, ffn1_norm[layer]), ffn1_w_in[layer], ffn1_w_out[layer])

        hp = rmsnorm(xp, mix_norm[layer])
        hs = rmsnorm(xs, mix_norm[layer])
        if layer < N_A_LAYERS:
            a = layer
            op, st_p = retention_prompt(hp, pos_p, ret_w_in[a], ret_gn[a], ret_w_out[a], log_gamma)
            os_, st_s = retention_sample(hs, pos_s, state_ret[a], ret_w_in[a], ret_gn[a],
                                         ret_w_out[a], log_gamma)
            ret_p.append(st_p)
            ret_s.append(st_s)
        else:
            bl = layer - N_A_LAYERS
            qp = sb_query(hp, sb_w_q[bl], sb_q_norm[bl])
            qs = sb_query(hs, sb_w_q[bl], sb_q_norm[bl])
            op = sb_prompt(qp, k_prompt, v_prompt, sb_bias[bl]).reshape(
                xp.shape[0], seq, SB_HEADS * SB_HD) @ sb_w_out[bl]
            os_ = stick_breaking(qs, k_all, v_all, qpos_s, kpos_s, sb_bias[bl]).reshape(
                dec_b, dec_s, SB_HEADS * SB_HD) @ sb_w_out[bl]
        xp = xp + op
        xs = xs + os_

        xp = xp + FFN_RES * swiglu(rmsnorm(xp, ffn2_norm[layer]), ffn2_w_in[layer], ffn2_w_out[layer])
        xs = xs + FFN_RES * swiglu(rmsnorm(xs, ffn2_norm[layer]), ffn2_w_in[layer], ffn2_w_out[layer])

    state_ret_prompt = jnp.stack(ret_p, axis=0)
    state_ret_sample = jnp.stack(ret_s, axis=0)
    return (xp, xs, state_ret_prompt, state_ret_sample, k_prompt, v_prompt, k_sample, v_sample)
```

```python
import functools
import math

import jax
import jax.numpy as jnp
from jax import lax
from jax.experimental import pallas as pl
from jax.experimental.pallas import tpu as pltpu

F32 = jnp.float32
BF16 = jnp.bfloat16

EPS = 1e-6
FFN_RES = 0.5
ROPE_BASE = 10000.0
RET_HEADS = 8
RET_CHUNK = 128
SB_HEADS = 16
SB_HD = 128

LANES = 128
VMEM_LIMIT_BYTES = 56 * 1024 * 1024


def _params(*semantics):
    return pltpu.CompilerParams(dimension_semantics=semantics, vmem_limit_bytes=VMEM_LIMIT_BYTES)


def _rmsnorm_rows(x, gain):
    return x * lax.rsqrt(jnp.mean(x * x, axis=-1, keepdims=True) + EPS) * gain


def _head_rmsnorm(y, gain, hd):
    outs = []
    for c in range(y.shape[1] // hd):
        blk = y[:, c * hd:(c + 1) * hd]
        outs.append(blk * lax.rsqrt(jnp.mean(blk * blk, axis=-1, keepdims=True) + EPS)
                    * gain[:, c * hd:(c + 1) * hd])
    return outs


def _dot(a, b):
    return jnp.dot(a, b, preferred_element_type=F32)


def _dot_nt(a, b):
    return lax.dot_general(a, b, (((1,), (1,)), ((), ())), preferred_element_type=F32)


def _dot_tn(a, b):
    return lax.dot_general(a, b, (((0,), (0,)), ((), ())), preferred_element_type=F32)


def _ffn_kernel(x_ref, g_ref, wg_ref, wu_ref, wo_ref, o_ref, h_ref):
    j = pl.program_id(1)

    @pl.when(j == 0)
    def _():
        h_ref[...] = _rmsnorm_rows(x_ref[...], g_ref[...]).astype(BF16)
        o_ref[...] = jnp.zeros_like(o_ref)

    h = h_ref[...]
    gate = _dot(h, wg_ref[...])
    up = _dot(h, wu_ref[...])
    act = (gate * jax.nn.sigmoid(gate) * up).astype(BF16)
    o_ref[...] += _dot(act, wo_ref[...])

    @pl.when(j == pl.num_programs(1) - 1)
    def _():
        o_ref[...] = x_ref[...] + FFN_RES * o_ref[...]


def _ffn(x, gain, w_in, w_out, *, tm, tf):
    m, d = x.shape
    dff = w_out.shape[0]
    nj = dff // tf
    assert m % tm == 0 and dff % tf == 0
    return pl.pallas_call(
        _ffn_kernel,
        out_shape=jax.ShapeDtypeStruct((m, d), F32),
        grid=(m // tm, nj),
        in_specs=[
            pl.BlockSpec((tm, d), lambda i, j: (i, 0)),
            pl.BlockSpec((1, d), lambda i, j: (0, 0)),
            pl.BlockSpec((d, tf), lambda i, j: (0, j)),
            pl.BlockSpec((d, tf), lambda i, j: (0, j + nj)),
            pl.BlockSpec((tf, d), lambda i, j: (j, 0)),
        ],
        out_specs=pl.BlockSpec((tm, d), lambda i, j: (i, 0)),
        scratch_shapes=[pltpu.VMEM((tm, d), BF16)],
        compiler_params=_params("parallel", "arbitrary"),
        name="ffn",
    )(x, gain.reshape(1, d), w_in, w_in, w_out)


def _ret_in_kernel(x_ref, g_ref, w_ref, cos_ref, sin_ref, o_ref, h_ref, *, nq, nk, dk, k_scale):
    j = pl.program_id(1)

    @pl.when(j == 0)
    def _():
        h_ref[...] = _rmsnorm_rows(x_ref[...], g_ref[...]).astype(BF16)

    y = _dot(h_ref[...], w_ref[...])
    half = dk // 2

    @pl.when(j < nq + nk)
    def _():
        sc = jnp.where(j < nq, 1.0, k_scale).astype(F32)
        cos = cos_ref[...]
        sin = sin_ref[...]
        for hh in range(y.shape[1] // dk):
            x1 = y[:, hh * dk:hh * dk + half]
            x2 = y[:, hh * dk + half:(hh + 1) * dk]
            o_ref[:, hh * dk:hh * dk + half] = ((x1 * cos - x2 * sin) * sc).astype(o_ref.dtype)
            o_ref[:, hh * dk + half:(hh + 1) * dk] = ((x1 * sin + x2 * cos) * sc).astype(o_ref.dtype)

    @pl.when(j >= nq + nk)
    def _():
        o_ref[...] = y.astype(o_ref.dtype)


def _ret_in(x, gain, w, cos, sin, *, tm, tn, out_dtype):
    m, d = x.shape
    n = w.shape[1]
    dk = d // RET_HEADS
    nper = cos.shape[0] // tm
    assert m % tm == 0 and n % tn == 0 and tn % dk == 0 and cos.shape[0] % tm == 0
    kern = functools.partial(_ret_in_kernel, nq=d // tn, nk=d // tn, dk=dk, k_scale=dk ** -0.5)
    return pl.pallas_call(
        kern,
        out_shape=jax.ShapeDtypeStruct((m, n), out_dtype),
        grid=(m // tm, n // tn),
        in_specs=[
            pl.BlockSpec((tm, d), lambda i, j: (i, 0)),
            pl.BlockSpec((1, d), lambda i, j: (0, 0)),
            pl.BlockSpec((d, tn), lambda i, j: (0, j)),
            pl.BlockSpec((tm, dk // 2), lambda i, j: (i % nper, 0)),
            pl.BlockSpec((tm, dk // 2), lambda i, j: (i % nper, 0)),
        ],
        out_specs=pl.BlockSpec((tm, tn), lambda i, j: (i, j)),
        scratch_shapes=[pltpu.VMEM((tm, d), BF16)],
        compiler_params=_params("parallel", "arbitrary"),
        name="ret_in",
    )(x, gain.reshape(1, d), w, cos, sin)


def _retention_heads(q_ref, k_ref, v_ref, g_ref, gn_ref, og_ref, get_state, put_state, *, c, dk, dv):
    row = lax.broadcasted_iota(jnp.int32, (c, c), 0)
    col = lax.broadcasted_iota(jnp.int32, (c, c), 1)
    diff = (row - col).astype(F32)
    idx = lax.broadcasted_iota(jnp.int32, (c, 1), 0).astype(F32)
    for h in range(RET_HEADS):
        lg = math.log(1.0 - 2.0 ** (-5.0 - h))
        q = q_ref[:, h * dk:(h + 1) * dk]
        k = k_ref[:, h * dk:(h + 1) * dk]
        v = v_ref[:, h * dv:(h + 1) * dv].astype(BF16)
        state = get_state(h)
        decay = jnp.where(diff >= 0, jnp.exp(lg * jnp.maximum(diff, 0.0)), 0.0)
        scores = _dot_nt(q.astype(BF16), k.astype(BF16)) * decay
        inner = _dot(scores.astype(BF16), v)
        q_decay = jnp.exp((idx + 1.0) * lg)
        cross = _dot((q.astype(F32) * q_decay).astype(BF16), state.astype(BF16))
        k_decay = jnp.exp((c - 1.0 - idx) * lg)
        kd = (k.astype(F32) * k_decay).astype(BF16)
        put_state(h, math.exp(c * lg) * state + _dot_tn(kd, v))
        o = inner + cross
        on = o * lax.rsqrt(jnp.mean(o * o, axis=-1, keepdims=True) + EPS) * gn_ref[:, h * dv:(h + 1) * dv]
        g = g_ref[:, h * dv:(h + 1) * dv].astype(F32)
        og_ref[:, h * dv:(h + 1) * dv] = (on * (g * jax.nn.sigmoid(g))).astype(og_ref.dtype)


def _ret_prompt_kernel(q_ref, k_ref, v_ref, g_ref, gn_ref, og_ref, st_ref, *, c, dk, dv):
    @pl.when(pl.program_id(1) == 0)
    def _():
        st_ref[...] = jnp.zeros_like(st_ref)

    def get_state(h):
        return st_ref[0, h]

    def put_state(h, s):
        st_ref[0, h] = s

    _retention_heads(q_ref, k_ref, v_ref, g_ref, gn_ref, og_ref, get_state, put_state, c=c, dk=dk, dv=dv)


def _ret_prompt(proj, gn, *, batch, seq):
    m, n = proj.shape
    d = n // 6
    dk, dv, c = d // RET_HEADS, 2 * d // RET_HEADS, RET_CHUNK
    nc = seq // c
    kern = functools.partial(_ret_prompt_kernel, c=c, dk=dk, dv=dv)
    og, st = pl.pallas_call(
        kern,
        out_shape=(jax.ShapeDtypeStruct((m, 2 * d), BF16),
                   jax.ShapeDtypeStruct((batch, RET_HEADS, dk, dv), F32)),
        grid=(batch, nc),
        in_specs=[
            pl.BlockSpec((c, d), lambda b, t: (b * nc + t, 0)),
            pl.BlockSpec((c, d), lambda b, t: (b * nc + t, 1)),
            pl.BlockSpec((c, 2 * d), lambda b, t: (b * nc + t, 1)),
            pl.BlockSpec((c, 2 * d), lambda b, t: (b * nc + t, 2)),
            pl.BlockSpec((1, 2 * d), lambda b, t: (0, 0)),
        ],
        out_specs=(pl.BlockSpec((c, 2 * d), lambda b, t: (b * nc + t, 0)),
                   pl.BlockSpec((1, RET_HEADS, dk, dv), lambda b, t: (b, 0, 0, 0))),
        compiler_params=_params("parallel", "arbitrary"),
        name="ret_prompt",
    )(proj, proj, proj, proj, gn.reshape(1, 2 * d))
    return og, st


def _ret_sample_kernel(q_ref, k_ref, v_ref, g_ref, gn_ref, st_in_ref, og_ref, st_ref, *, c, dk, dv):
    def get_state(h):
        return st_in_ref[0, h]

    def put_state(h, s):
        st_ref[0, h] = s

    _retention_heads(q_ref.at[0], k_ref.at[0], v_ref.at[0], g_ref.at[0], gn_ref, og_ref.at[0],
                     get_state, put_state, c=c, dk=dk, dv=dv)


def _ret_sample(proj, state, gn):
    batch, c, n = proj.shape
    d = n // 6
    dk, dv = d // RET_HEADS, 2 * d // RET_HEADS
    kern = functools.partial(_ret_sample_kernel, c=c, dk=dk, dv=dv)
    og, st = pl.pallas_call(
        kern,
        out_shape=(jax.ShapeDtypeStruct((batch, c, 2 * d), F32),
                   jax.ShapeDtypeStruct((batch, RET_HEADS, dk, dv), F32)),
        grid=(batch,),
        in_specs=[
            pl.BlockSpec((1, c, d), lambda b: (b, 0, 0)),
            pl.BlockSpec((1, c, d), lambda b: (b, 0, 1)),
            pl.BlockSpec((1, c, 2 * d), lambda b: (b, 0, 1)),
            pl.BlockSpec((1, c, 2 * d), lambda b: (b, 0, 2)),
            pl.BlockSpec((1, 2 * d), lambda b: (0, 0)),
            pl.BlockSpec((1, RET_HEADS, dk, dv), lambda b: (b, 0, 0, 0)),
        ],
        out_specs=(pl.BlockSpec((1, c, 2 * d), lambda b: (b, 0, 0)),
                   pl.BlockSpec((1, RET_HEADS, dk, dv), lambda b: (b, 0, 0, 0))),
        compiler_params=_params("parallel"),
        name="ret_sample",
    )(proj, proj, proj, proj, gn.reshape(1, 2 * d), state)
    return og, st


def _matmul_res_kernel(a_ref, w_ref, r_ref, o_ref):
    o_ref[...] = r_ref[...] + _dot(a_ref[...].astype(BF16), w_ref[...])


def _matmul_res(a, w, res, *, tm, tn):
    m, k = a.shape
    n = w.shape[1]
    assert m % tm == 0 and n % tn == 0
    return pl.pallas_call(
        _matmul_res_kernel,
        out_shape=jax.ShapeDtypeStruct((m, n), F32),
        grid=(m // tm, n // tn),
        in_specs=[
            pl.BlockSpec((tm, k), lambda i, j: (i, 0)),
            pl.BlockSpec((k, tn), lambda i, j: (0, j)),
            pl.BlockSpec((tm, tn), lambda i, j: (i, j)),
        ],
        out_specs=pl.BlockSpec((tm, tn), lambda i, j: (i, j)),
        compiler_params=_params("parallel", "arbitrary"),
        name="matmul_res",
    )(a, w, res)


def _kv_kernel(x_ref, g_ref, wk_ref, wv_ref, kn_ref, k_ref, v_ref, kb_ref, vb_ref, h_ref):
    @pl.when(pl.program_id(1) == 0)
    def _():
        h_ref[...] = _rmsnorm_rows(x_ref[...], g_ref[...]).astype(BF16)

    h = h_ref[...]
    yk = _dot(h, wk_ref[...])
    for c, blk in enumerate(_head_rmsnorm(yk, kn_ref[...], SB_HD)):
        k_ref[:, c * SB_HD:(c + 1) * SB_HD] = blk
        kb_ref[:, c * SB_HD:(c + 1) * SB_HD] = blk.astype(BF16)
    yv = _dot(h, wv_ref[...])
    v_ref[...] = yv
    vb_ref[...] = yv.astype(BF16)


def _shared_kv(x, gain, w_kv, k_norm, *, tm, tn):
    m, d = x.shape
    n = w_kv.shape[1] // 2
    nj = n // tn
    assert m % tm == 0 and n % tn == 0 and tn % SB_HD == 0
    blk = pl.BlockSpec((tm, tn), lambda i, j: (i, j))
    return pl.pallas_call(
        _kv_kernel,
        out_shape=(jax.ShapeDtypeStruct((m, n), F32), jax.ShapeDtypeStruct((m, n), F32),
                   jax.ShapeDtypeStruct((m, n), BF16), jax.ShapeDtypeStruct((m, n), BF16)),
        grid=(m // tm, nj),
        in_specs=[
            pl.BlockSpec((tm, d), lambda i, j: (i, 0)),
            pl.BlockSpec((1, d), lambda i, j: (0, 0)),
            pl.BlockSpec((d, tn), lambda i, j: (0, j)),
            pl.BlockSpec((d, tn), lambda i, j: (0, j + nj)),
            pl.BlockSpec((1, tn), lambda i, j: (0, j)),
        ],
        out_specs=(blk, blk, blk, blk),
        scratch_shapes=[pltpu.VMEM((tm, d), BF16)],
        compiler_params=_params("parallel", "arbitrary"),
        name="shared_kv",
    )(x, gain.reshape(1, d), w_kv, w_kv, k_norm.reshape(1, n))


def _sbq_kernel(x_ref, g_ref, w_ref, qn_ref, o_ref, h_ref):
    @pl.when(pl.program_id(1) == 0)
    def _():
        h_ref[...] = _rmsnorm_rows(x_ref[...], g_ref[...]).astype(BF16)

    y = _dot(h_ref[...], w_ref[...])
    for c, blk in enumerate(_head_rmsnorm(y, qn_ref[...], SB_HD)):
        o_ref[:, c * SB_HD:(c + 1) * SB_HD] = blk.astype(o_ref.dtype)


def _sb_query(x, gain, w_q, q_norm, *, tm, tn, out_dtype):
    m, d = x.shape
    n = w_q.shape[1]
    assert m % tm == 0 and n % tn == 0 and tn % SB_HD == 0
    return pl.pallas_call(
        _sbq_kernel,
        out_shape=jax.ShapeDtypeStruct((m, n), out_dtype),
        grid=(m // tm, n // tn),
        in_specs=[
            pl.BlockSpec((tm, d), lambda i, j: (i, 0)),
            pl.BlockSpec((1, d), lambda i, j: (0, 0)),
            pl.BlockSpec((d, tn), lambda i, j: (0, j)),
            pl.BlockSpec((1, tn), lambda i, j: (0, j)),
        ],
        out_specs=pl.BlockSpec((tm, tn), lambda i, j: (i, j)),
        scratch_shapes=[pltpu.VMEM((tm, d), BF16)],
        compiler_params=_params("parallel", "arbitrary"),
        name="sb_query",
    )(x, gain.reshape(1, d), w_q, q_norm.reshape(1, n))


def _suffix_matrix(tk):
    j = lax.broadcasted_iota(jnp.int32, (tk, tk + LANES), 0)
    s = lax.broadcasted_iota(jnp.int32, (tk, tk + LANES), 1)
    return jnp.where((j > s) | (s >= tk), 1.0, 0.0).astype(BF16)


def _split_bf16(x):
    hi = x.astype(BF16)
    return hi, (x - hi.astype(F32)).astype(BF16)


def _sb_weights(qk, bias, carry, suffix, mask):
    tk = qk.shape[1]
    z = qk * (SB_HD ** -0.5) + bias
    t = jnp.log(1.0 + jnp.exp(-jnp.abs(z)))
    log_beta = jnp.minimum(z, 0.0) - t
    log_keep = jnp.minimum(-z, 0.0) - t
    if mask is not None:
        log_keep = jnp.where(mask, log_keep, 0.0)
    hi, lo = _split_bf16(log_keep)
    r = _dot(hi, suffix) + _dot(lo, suffix)
    tile_total = r[:, tk:]
    reps = tk // LANES
    carry_w = carry if reps == 1 else jnp.concatenate([carry] * reps, axis=1)
    a = jnp.exp(log_beta + r[:, :tk] + carry_w)
    if mask is not None:
        a = jnp.where(mask, a, 0.0)
    return a.astype(BF16), carry + tile_total


def _sb_tile(q, k, v, bias, carry, suffix, mask):
    a, carry = _sb_weights(_dot_nt(q, k), bias, carry, suffix, mask)
    return _dot(a, v), carry


def _sb_prompt_kernel(bias_ref, q_ref, k_ref, v_ref, o_ref, acc_ref, carry_ref, *, tq, heads):
    hg = pl.program_id(1)
    i = pl.program_id(2)
    suffix = _suffix_matrix(tq)
    row = lax.broadcasted_iota(jnp.int32, (tq, tq), 0)
    col = lax.broadcasted_iota(jnp.int32, (tq, tq), 1)
    diag_mask = col < row

    def tile(j, mask, first):
        start = pl.multiple_of(j * tq, tq)
        for hh in range(heads):
            cs = slice(hh * SB_HD, (hh + 1) * SB_HD)
            bias = bias_ref[hg * heads + hh]
            carry = jnp.zeros((tq, LANES), F32) if first else carry_ref[hh]
            av, carry = _sb_tile(q_ref[:, cs], k_ref[pl.ds(start, tq), cs], v_ref[pl.ds(start, tq), cs],
                                 bias, carry, suffix, mask)
            carry_ref[hh] = carry
            if first:
                acc_ref[hh] = av
            else:
                acc_ref[hh] += av

    tile(i, diag_mask, True)

    def body(t, _):
        tile(i - 1 - t, None, False)
        return 0

    lax.fori_loop(0, i, body, 0)
    for hh in range(heads):
        o_ref[:, hh * SB_HD:(hh + 1) * SB_HD] = acc_ref[hh].astype(o_ref.dtype)


def _sb_prompt(q, k, v, bias, *, batch, seq, tq, heads):
    m, n = q.shape
    nq = seq // tq
    w = heads * SB_HD
    assert seq % tq == 0 and n % w == 0 and tq % LANES == 0
    kern = functools.partial(_sb_prompt_kernel, tq=tq, heads=heads)
    return pl.pallas_call(
        kern,
        out_shape=jax.ShapeDtypeStruct((m, n), BF16),
        grid=(batch, n // w, nq),
        in_specs=[
            pl.BlockSpec(memory_space=pltpu.SMEM),
            pl.BlockSpec((tq, w), lambda b, h, i: (b * nq + i, h)),
            pl.BlockSpec((seq, w), lambda b, h, i: (b, h)),
            pl.BlockSpec((seq, w), lambda b, h, i: (b, h)),
        ],
        out_specs=pl.BlockSpec((tq, w), lambda b, h, i: (b * nq + i, h)),
        scratch_shapes=[pltpu.VMEM((heads, tq, SB_HD), F32), pltpu.VMEM((heads, tq, LANES), F32)],
        compiler_params=_params("parallel", "parallel", "arbitrary"),
        name="sb_prompt",
    )(bias, q, k, v)


def _sb_sample_kernel(pt_ref, q_ref, kn_ref, vn_ref, bias_ref, own_ref, fold_ref, spread_ref, *refs,
                      c, pages_per_step, page):
    k_refs = refs[:pages_per_step]
    v_refs = refs[pages_per_step:2 * pages_per_step]
    o_ref = refs[2 * pages_per_step]
    kpad_ref, vpad_ref, acc_ref, carry_ref = refs[2 * pages_per_step + 1:]
    jj = pl.program_id(1)
    rows = SB_HEADS * c
    suffix = _suffix_matrix(page)
    bias = bias_ref[...]
    own = own_ref[...]
    q = q_ref[0].astype(BF16)

    def tile(pk, pv, carry, mask):
        wide = _dot_nt(q, pk) * own
        hi, lo = _split_bf16(wide)
        qk = _dot(hi, fold_ref[...]) + _dot(lo, fold_ref[...])
        a, carry = _sb_weights(qk, bias, carry, suffix, mask)
        a_wide = (_dot(a, spread_ref[...]) * own).astype(BF16)
        return _dot(a_wide, pv), carry

    @pl.when(jj == 0)
    def _():
        kpad_ref[...] = jnp.zeros_like(kpad_ref)
        vpad_ref[...] = jnp.zeros_like(vpad_ref)
        kpad_ref[0:c * SB_HEADS, :] = kn_ref[0]
        vpad_ref[0:c * SB_HEADS, :] = vn_ref[0]
        r_idx = lax.broadcasted_iota(jnp.int32, (rows, page), 0)
        s_idx = lax.broadcasted_iota(jnp.int32, (rows, page), 1)
        av, carry = tile(kpad_ref[...].astype(BF16), vpad_ref[...].astype(BF16),
                         jnp.zeros((rows, LANES), F32), s_idx < (r_idx % c))
        acc_ref[...] = av
        carry_ref[...] = carry

    carry = carry_ref[...]
    acc = acc_ref[...]
    for p in range(pages_per_step):
        av, carry = tile(k_refs[p][0].astype(BF16), v_refs[p][0].astype(BF16), carry, None)
        acc = acc + av
    carry_ref[...] = carry
    acc_ref[...] = acc

    @pl.when(jj == pl.num_programs(1) - 1)
    def _():
        o_ref[0] = acc_ref[...]


def _sb_sample(q, k_new, v_new, cache_k, cache_v, page_table, bias, *, pages_per_step):
    batch, rows, hd = q.shape
    c = rows // SB_HEADS
    n_pages = page_table.shape[1]
    prow = cache_k.shape[1]
    page = prow // SB_HEADS
    steps = n_pages // pages_per_step
    assert n_pages % pages_per_step == 0 and page % LANES == 0
    bias_rows = jnp.broadcast_to(jnp.repeat(bias.astype(F32), c)[:, None], (rows, page))
    r_head = jnp.arange(rows)[:, None] // c
    col = jnp.arange(prow)[None, :]
    own = (col % SB_HEADS == r_head).astype(F32)
    fold = (jnp.arange(prow)[:, None] // SB_HEADS == jnp.arange(page)[None, :]).astype(BF16)
    spread = fold.T

    def page_spec(p):
        return pl.BlockSpec((1, prow, hd),
                            lambda b, jj, pt: (pt[b * n_pages + n_pages - 1 - (jj * pages_per_step + p)], 0, 0))

    const = lambda shape: pl.BlockSpec(shape, lambda b, jj, pt: (0,) * len(shape))
    kern = functools.partial(_sb_sample_kernel, c=c, pages_per_step=pages_per_step, page=page)
    grid_spec = pltpu.PrefetchScalarGridSpec(
        num_scalar_prefetch=1,
        grid=(batch, steps),
        in_specs=[pl.BlockSpec((1, rows, hd), lambda b, jj, pt: (b, 0, 0)),
                  pl.BlockSpec((1, c * SB_HEADS, hd), lambda b, jj, pt: (b, 0, 0)),
                  pl.BlockSpec((1, c * SB_HEADS, hd), lambda b, jj, pt: (b, 0, 0)),
                  const((rows, page)), const((rows, prow)), const((prow, page)), const((page, prow))]
        + [page_spec(p) for p in range(pages_per_step)] * 2,
        out_specs=pl.BlockSpec((1, rows, hd), lambda b, jj, pt: (b, 0, 0)),
        scratch_shapes=[pltpu.VMEM((prow, hd), F32), pltpu.VMEM((prow, hd), F32),
                        pltpu.VMEM((rows, hd), F32), pltpu.VMEM((rows, LANES), F32)],
    )
    return pl.pallas_call(
        kern,
        out_shape=jax.ShapeDtypeStruct((batch, rows, hd), F32),
        grid_spec=grid_spec,
        compiler_params=_params("parallel", "arbitrary"),
        name="sb_sample",
    )(page_table.reshape(-1), q, k_new, v_new, bias_rows, own, fold, spread,
      *([cache_k] * pages_per_step), *([cache_v] * pages_per_step))


def _rope_tables(pos, half):
    inv = ROPE_BASE ** (-jnp.arange(half, dtype=F32) / half)
    ang = pos[:, None] * inv[None, :]
    return jnp.cos(ang), jnp.sin(ang)


def kernel(x_prompt, x_sample, state_ret, cache_k, cache_v, page_table, ffn1_norm, ffn1_w_in, ffn1_w_out, mix_norm, ffn2_norm, ffn2_w_in, ffn2_w_out, ret_w_in, ret_gn, ret_w_out, kv_norm, w_kv, k_norm, sb_w_q, sb_q_norm, sb_bias, sb_w_out):
    batch, seq, d = x_prompt.shape
    dec_b, dec_s, _ = x_sample.shape
    n_pool, page = cache_k.shape[:2]
    past_len = page_table.shape[1] * page
    depth = ffn1_norm.shape[0]
    n_a = ret_w_in.shape[0]
    dk = d // RET_HEADS
    mp, ms = batch * seq, dec_b * dec_s

    bf = lambda w: w.astype(BF16)
    ffn1_w_in, ffn1_w_out, ffn2_w_in, ffn2_w_out = bf(ffn1_w_in), bf(ffn1_w_out), bf(ffn2_w_in), bf(ffn2_w_out)
    ret_w_in, ret_w_out, w_kv, sb_w_q, sb_w_out = bf(ret_w_in), bf(ret_w_out), bf(w_kv), bf(sb_w_q), bf(sb_w_out)

    tm_p, tm_s = 512, ms
    tm_big = min(1024, mp)
    cos_p, sin_p = _rope_tables(jnp.arange(seq, dtype=F32), dk // 2)
    cos_s, sin_s = _rope_tables(past_len + jnp.arange(dec_s, dtype=F32), dk // 2)
    cos_s, sin_s = jnp.tile(cos_s, (dec_b, 1)), jnp.tile(sin_s, (dec_b, 1))

    xp = x_prompt.reshape(mp, d)
    xs = x_sample.reshape(ms, d)
    cache_k2 = cache_k.reshape(n_pool, page * SB_HEADS, SB_HD)
    cache_v2 = cache_v.reshape(n_pool, page * SB_HEADS, SB_HD)

    def ffn_pair(xp, xs, norm, w_in, w_out):
        return (_ffn(xp, norm, w_in, w_out, tm=tm_p, tf=512),
                _ffn(xs, norm, w_in, w_out, tm=tm_s, tf=512))

    ret_p, ret_s = [], []
    k_p = v_p = k_s = v_s = kb_p = vb_p = None
    for layer in range(depth):
        if layer == n_a:
            k_p, v_p, kb_p, vb_p = _shared_kv(xp, kv_norm, w_kv, k_norm, tm=tm_p, tn=512)
            k_s, v_s, _, _ = _shared_kv(xs, kv_norm, w_kv, k_norm, tm=tm_s, tn=512)

        xp, xs = ffn_pair(xp, xs, ffn1_norm[layer], ffn1_w_in[layer], ffn1_w_out[layer])

        if layer < n_a:
            a = layer
            proj_p = _ret_in(xp, mix_norm[layer], ret_w_in[a], cos_p, sin_p, tm=tm_p, tn=512, out_dtype=BF16)
            proj_s = _ret_in(xs, mix_norm[layer], ret_w_in[a], cos_s, sin_s, tm=tm_s, tn=512, out_dtype=F32)
            og_p, st_p = _ret_prompt(proj_p, ret_gn[a], batch=batch, seq=seq)
            og_s, st_s = _ret_sample(proj_s.reshape(dec_b, dec_s, -1), state_ret[a], ret_gn[a])
            ret_p.append(st_p)
            ret_s.append(st_s)
            xp = _matmul_res(og_p, ret_w_out[a], xp, tm=tm_big, tn=512)
            xs = _matmul_res(og_s.reshape(ms, -1), ret_w_out[a], xs, tm=tm_s, tn=512)
        else:
            bl = layer - n_a
            q_p = _sb_query(xp, mix_norm[layer], sb_w_q[bl], sb_q_norm[bl], tm=tm_p, tn=512, out_dtype=BF16)
            q_s = _sb_query(xs, mix_norm[layer], sb_w_q[bl], sb_q_norm[bl], tm=tm_s, tn=512, out_dtype=F32)
            at_p = _sb_prompt(q_p, kb_p, vb_p, sb_bias[bl], batch=batch, seq=seq, tq=256, heads=2)
            q_hq = q_s.reshape(dec_b, dec_s, SB_HEADS, SB_HD).transpose(0, 2, 1, 3)
            at_hq = _sb_sample(q_hq.reshape(dec_b, SB_HEADS * dec_s, SB_HD),
                               k_s.reshape(dec_b, dec_s * SB_HEADS, SB_HD),
                               v_s.reshape(dec_b, dec_s * SB_HEADS, SB_HD),
                               cache_k2, cache_v2, page_table, sb_bias[bl], pages_per_step=4)
            at_s = at_hq.reshape(dec_b, SB_HEADS, dec_s, SB_HD).transpose(0, 2, 1, 3).reshape(ms, d)
            xp = _matmul_res(at_p, sb_w_out[bl], xp, tm=tm_big, tn=512)
            xs = _matmul_res(at_s, sb_w_out[bl], xs, tm=tm_s, tn=512)

        xp, xs = ffn_pair(xp, xs, ffn2_norm[layer], ffn2_w_in[layer], ffn2_w_out[layer])

    hd_shape = (SB_HEADS, SB_HD)
    return (xp.reshape(batch, seq, d), xs.reshape(dec_b, dec_s, d),
            jnp.stack(ret_p, axis=0), jnp.stack(ret_s, axis=0),
            k_p.reshape(batch, seq, *hd_shape), v_p.reshape(batch, seq, *hd_shape),
            k_s.reshape(dec_b, dec_s, *hd_shape), v_s.reshape(dec_b, dec_s, *hd_shape))
```

```python
import functools
import math

import jax
import jax.numpy as jnp
from jax import lax
from jax.experimental import pallas as pl
from jax.experimental.pallas import tpu as pltpu

F32 = jnp.float32
BF16 = jnp.bfloat16

EPS = 1e-6
FFN_RES = 0.5
ROPE_BASE = 10000.0
RET_HEADS = 8
RET_CHUNK = 128
SB_HEADS = 16
SB_HD = 128

LANES = 128
VMEM_LIMIT_BYTES = 56 * 1024 * 1024


def _params(*semantics):
    return pltpu.CompilerParams(dimension_semantics=semantics, vmem_limit_bytes=VMEM_LIMIT_BYTES)


def _rmsnorm_rows(x, gain):
    return x * lax.rsqrt(jnp.mean(x * x, axis=-1, keepdims=True) + EPS) * gain


def _head_rmsnorm(y, gain, hd):
    outs = []
    for c in range(y.shape[1] // hd):
        blk = y[:, c * hd:(c + 1) * hd]
        outs.append(blk * lax.rsqrt(jnp.mean(blk * blk, axis=-1, keepdims=True) + EPS)
                    * gain[:, c * hd:(c + 1) * hd])
    return outs


def _dot(a, b):
    return jnp.dot(a, b, preferred_element_type=F32)


def _dot_nt(a, b):
    return lax.dot_general(a, b, (((1,), (1,)), ((), ())), preferred_element_type=F32)


def _dot_tn(a, b):
    return lax.dot_general(a, b, (((0,), (0,)), ((), ())), preferred_element_type=F32)


def _ffn_kernel(x_ref, g_ref, wg_ref, wu_ref, wo_ref, o_ref, h_ref):
    j = pl.program_id(1)

    @pl.when(j == 0)
    def _():
        h_ref[...] = _rmsnorm_rows(x_ref[...], g_ref[...]).astype(BF16)
        o_ref[...] = jnp.zeros_like(o_ref)

    h = h_ref[...]
    gate = _dot(h, wg_ref[...])
    up = _dot(h, wu_ref[...])
    act = (gate * jax.nn.sigmoid(gate) * up).astype(BF16)
    o_ref[...] += _dot(act, wo_ref[...])

    @pl.when(j == pl.num_programs(1) - 1)
    def _():
        o_ref[...] = x_ref[...] + FFN_RES * o_ref[...]


def _ffn(x, gain, w_in, w_out, *, tm, tf):
    m, d = x.shape
    dff = w_out.shape[0]
    nj = dff // tf
    assert m % tm == 0 and dff % tf == 0
    return pl.pallas_call(
        _ffn_kernel,
        out_shape=jax.ShapeDtypeStruct((m, d), F32),
        grid=(m // tm, nj),
        in_specs=[
            pl.BlockSpec((tm, d), lambda i, j: (i, 0)),
            pl.BlockSpec((1, d), lambda i, j: (0, 0)),
            pl.BlockSpec((d, tf), lambda i, j: (0, j)),
            pl.BlockSpec((d, tf), lambda i, j: (0, j + nj)),
            pl.BlockSpec((tf, d), lambda i, j: (j, 0)),
        ],
        out_specs=pl.BlockSpec((tm, d), lambda i, j: (i, 0)),
        scratch_shapes=[pltpu.VMEM((tm, d), BF16)],
        compiler_params=_params("parallel", "arbitrary"),
        name="ffn",
    )(x, gain.reshape(1, d), w_in, w_in, w_out)


def _ret_in_kernel(x_ref, g_ref, w_ref, cos_ref, sin_ref, o_ref, h_ref, *, nq, nk, dk, k_scale):
    j = pl.program_id(1)

    @pl.when(j == 0)
    def _():
        h_ref[...] = _rmsnorm_rows(x_ref[...], g_ref[...]).astype(BF16)

    y = _dot(h_ref[...], w_ref[...])
    half = dk // 2

    @pl.when(j < nq + nk)
    def _():
        sc = jnp.where(j < nq, 1.0, k_scale).astype(F32)
        cos = cos_ref[...]
        sin = sin_ref[...]
        for hh in range(y.shape[1] // dk):
            x1 = y[:, hh * dk:hh * dk + half]
            x2 = y[:, hh * dk + half:(hh + 1) * dk]
            o_ref[:, hh * dk:hh * dk + half] = ((x1 * cos - x2 * sin) * sc).astype(o_ref.dtype)
            o_ref[:, hh * dk + half:(hh + 1) * dk] = ((x1 * sin + x2 * cos) * sc).astype(o_ref.dtype)

    @pl.when(j >= nq + nk)
    def _():
        o_ref[...] = y.astype(o_ref.dtype)


def _ret_in(x, gain, w, cos, sin, *, tm, tn, out_dtype):
    m, d = x.shape
    n = w.shape[1]
    dk = d // RET_HEADS
    nper = cos.shape[0] // tm
    assert m % tm == 0 and n % tn == 0 and tn % dk == 0 and cos.shape[0] % tm == 0
    kern = functools.partial(_ret_in_kernel, nq=d // tn, nk=d // tn, dk=dk, k_scale=dk ** -0.5)
    return pl.pallas_call(
        kern,
        out_shape=jax.ShapeDtypeStruct((m, n), out_dtype),
        grid=(m // tm, n // tn),
        in_specs=[
            pl.BlockSpec((tm, d), lambda i, j: (i, 0)),
            pl.BlockSpec((1, d), lambda i, j: (0, 0)),
            pl.BlockSpec((d, tn), lambda i, j: (0, j)),
            pl.BlockSpec((tm, dk // 2), lambda i, j: (i % nper, 0)),
            pl.BlockSpec((tm, dk // 2), lambda i, j: (i % nper, 0)),
        ],
        out_specs=pl.BlockSpec((tm, tn), lambda i, j: (i, j)),
        scratch_shapes=[pltpu.VMEM((tm, d), BF16)],
        compiler_params=_params("parallel", "arbitrary"),
        name="ret_in",
    )(x, gain.reshape(1, d), w, cos, sin)


def _retention_heads(q_ref, k_ref, v_ref, g_ref, gn_ref, og_ref, get_state, put_state, *, c, dk, dv):
    row = lax.broadcasted_iota(jnp.int32, (c, c), 0)
    col = lax.broadcasted_iota(jnp.int32, (c, c), 1)
    diff = (row - col).astype(F32)
    idx = lax.broadcasted_iota(jnp.int32, (c, 1), 0).astype(F32)
    for h in range(RET_HEADS):
        lg = math.log(1.0 - 2.0 ** (-5.0 - h))
        q = q_ref[:, h * dk:(h + 1) * dk]
        k = k_ref[:, h * dk:(h + 1) * dk]
        v = v_ref[:, h * dv:(h + 1) * dv].astype(BF16)
        state = get_state(h)
        decay = jnp.where(diff >= 0, jnp.exp(lg * jnp.maximum(diff, 0.0)), 0.0)
        scores = _dot_nt(q.astype(BF16), k.astype(BF16)) * decay
        inner = _dot(scores.astype(BF16), v)
        q_decay = jnp.exp((idx + 1.0) * lg)
        cross = _dot((q.astype(F32) * q_decay).astype(BF16), state.astype(BF16))
        k_decay = jnp.exp((c - 1.0 - idx) * lg)
        kd = (k.astype(F32) * k_decay).astype(BF16)
        put_state(h, math.exp(c * lg) * state + _dot_tn(kd, v))
        o = inner + cross
        on = o * lax.rsqrt(jnp.mean(o * o, axis=-1, keepdims=True) + EPS) * gn_ref[:, h * dv:(h + 1) * dv]
        g = g_ref[:, h * dv:(h + 1) * dv].astype(F32)
        og_ref[:, h * dv:(h + 1) * dv] = (on * (g * jax.nn.sigmoid(g))).astype(og_ref.dtype)


def _ret_prompt_kernel(q_ref, k_ref, v_ref, g_ref, gn_ref, og_ref, st_ref, *, c, dk, dv):
    @pl.when(pl.program_id(1) == 0)
    def _():
        st_ref[...] = jnp.zeros_like(st_ref)

    def get_state(h):
        return st_ref[0, h]

    def put_state(h, s):
        st_ref[0, h] = s

    _retention_heads(q_ref, k_ref, v_ref, g_ref, gn_ref, og_ref, get_state, put_state, c=c, dk=dk, dv=dv)


def _ret_prompt(proj, gn, *, batch, seq):
    m, n = proj.shape
    d = n // 6
    dk, dv, c = d // RET_HEADS, 2 * d // RET_HEADS, RET_CHUNK
    nc = seq // c
    kern = functools.partial(_ret_prompt_kernel, c=c, dk=dk, dv=dv)
    og, st = pl.pallas_call(
        kern,
        out_shape=(jax.ShapeDtypeStruct((m, 2 * d), BF16),
                   jax.ShapeDtypeStruct((batch, RET_HEADS, dk, dv), F32)),
        grid=(batch, nc),
        in_specs=[
            pl.BlockSpec((c, d), lambda b, t: (b * nc + t, 0)),
            pl.BlockSpec((c, d), lambda b, t: (b * nc + t, 1)),
            pl.BlockSpec((c, 2 * d), lambda b, t: (b * nc + t, 1)),
            pl.BlockSpec((c, 2 * d), lambda b, t: (b * nc + t, 2)),
            pl.BlockSpec((1, 2 * d), lambda b, t: (0, 0)),
        ],
        out_specs=(pl.BlockSpec((c, 2 * d), lambda b, t: (b * nc + t, 0)),
                   pl.BlockSpec((1, RET_HEADS, dk, dv), lambda b, t: (b, 0, 0, 0))),
        compiler_params=_params("parallel", "arbitrary"),
        name="ret_prompt",
    )(proj, proj, proj, proj, gn.reshape(1, 2 * d))
    return og, st


def _ret_sample_kernel(q_ref, k_ref, v_ref, g_ref, gn_ref, st_in_ref, og_ref, st_ref, *, c, dk, dv):
    def get_state(h):
        return st_in_ref[0, h]

    def put_state(h, s):
        st_ref[0, h] = s

    _retention_heads(q_ref.at[0], k_ref.at[0], v_ref.at[0], g_ref.at[0], gn_ref, og_ref.at[0],
                     get_state, put_state, c=c, dk=dk, dv=dv)


def _ret_sample(proj, state, gn):
    batch, c, n = proj.shape
    d = n // 6
    dk, dv = d // RET_HEADS, 2 * d // RET_HEADS
    kern = functools.partial(_ret_sample_kernel, c=c, dk=dk, dv=dv)
    og, st = pl.pallas_call(
        kern,
        out_shape=(jax.ShapeDtypeStruct((batch, c, 2 * d), F32),
                   jax.ShapeDtypeStruct((batch, RET_HEADS, dk, dv), F32)),
        grid=(batch,),
        in_specs=[
            pl.BlockSpec((1, c, d), lambda b: (b, 0, 0)),
            pl.BlockSpec((1, c, d), lambda b: (b, 0, 1)),
            pl.BlockSpec((1, c, 2 * d), lambda b: (b, 0, 1)),
            pl.BlockSpec((1, c, 2 * d), lambda b: (b, 0, 2)),
            pl.BlockSpec((1, 2 * d), lambda b: (0, 0)),
            pl.BlockSpec((1, RET_HEADS, dk, dv), lambda b: (b, 0, 0, 0)),
        ],
        out_specs=(pl.BlockSpec((1, c, 2 * d), lambda b: (b, 0, 0)),
                   pl.BlockSpec((1, RET_HEADS, dk, dv), lambda b: (b, 0, 0, 0))),
        compiler_params=_params("parallel"),
        name="ret_sample",
    )(proj, proj, proj, proj, gn.reshape(1, 2 * d), state)
    return og, st


def _matmul_res_kernel(a_ref, w_ref, r_ref, o_ref):
    o_ref[...] = r_ref[...] + _dot(a_ref[...].astype(BF16), w_ref[...])


def _matmul_res(a, w, res, *, tm, tn):
    m, k = a.shape
    n = w.shape[1]
    assert m % tm == 0 and n % tn == 0
    return pl.pallas_call(
        _matmul_res_kernel,
        out_shape=jax.ShapeDtypeStruct((m, n), F32),
        grid=(m // tm, n // tn),
        in_specs=[
            pl.BlockSpec((tm, k), lambda i, j: (i, 0)),
            pl.BlockSpec((k, tn), lambda i, j: (0, j)),
            pl.BlockSpec((tm, tn), lambda i, j: (i, j)),
        ],
        out_specs=pl.BlockSpec((tm, tn), lambda i, j: (i, j)),
        compiler_params=_params("parallel", "arbitrary"),
        name="matmul_res",
    )(a, w, res)


def _kv_kernel(x_ref, g_ref, wk_ref, wv_ref, kn_ref, k_ref, v_ref, kb_ref, vb_ref, h_ref):
    @pl.when(pl.program_id(1) == 0)
    def _():
        h_ref[...] = _rmsnorm_rows(x_ref[...], g_ref[...]).astype(BF16)

    h = h_ref[...]
    yk = _dot(h, wk_ref[...])
    for c, blk in enumerate(_head_rmsnorm(yk, kn_ref[...], SB_HD)):
        k_ref[:, c * SB_HD:(c + 1) * SB_HD] = blk
        kb_ref[:, c * SB_HD:(c + 1) * SB_HD] = blk.astype(BF16)
    yv = _dot(h, wv_ref[...])
    v_ref[...] = yv
    vb_ref[...] = yv.astype(BF16)


def _shared_kv(x, gain, w_kv, k_norm, *, tm, tn):
    m, d = x.shape
    n = w_kv.shape[1] // 2
    nj = n // tn
    assert m % tm == 0 and n % tn == 0 and tn % SB_HD == 0
    blk = pl.BlockSpec((tm, tn), lambda i, j: (i, j))
    return pl.pallas_call(
        _kv_kernel,
        out_shape=(jax.ShapeDtypeStruct((m, n), F32), jax.ShapeDtypeStruct((m, n), F32),
                   jax.ShapeDtypeStruct((m, n), BF16), jax.ShapeDtypeStruct((m, n), BF16)),
        grid=(m // tm, nj),
        in_specs=[
            pl.BlockSpec((tm, d), lambda i, j: (i, 0)),
            pl.BlockSpec((1, d), lambda i, j: (0, 0)),
            pl.BlockSpec((d, tn), lambda i, j: (0, j)),
            pl.BlockSpec((d, tn), lambda i, j: (0, j + nj)),
            pl.BlockSpec((1, tn), lambda i, j: (0, j)),
        ],
        out_specs=(blk, blk, blk, blk),
        scratch_shapes=[pltpu.VMEM((tm, d), BF16)],
        compiler_params=_params("parallel", "arbitrary"),
        name="shared_kv",
    )(x, gain.reshape(1, d), w_kv, w_kv, k_norm.reshape(1, n))


def _sbq_kernel(x_ref, g_ref, w_ref, qn_ref, o_ref, h_ref):
    @pl.when(pl.program_id(1) == 0)
    def _():
        h_ref[...] = _rmsnorm_rows(x_ref[...], g_ref[...]).astype(BF16)

    y = _dot(h_ref[...], w_ref[...])
    for c, blk in enumerate(_head_rmsnorm(y, qn_ref[...], SB_HD)):
        o_ref[:, c * SB_HD:(c + 1) * SB_HD] = blk.astype(o_ref.dtype)


def _sb_query(x, gain, w_q, q_norm, *, tm, tn, out_dtype):
    m, d = x.shape
    n = w_q.shape[1]
    assert m % tm == 0 and n % tn == 0 and tn % SB_HD == 0
    return pl.pallas_call(
        _sbq_kernel,
        out_shape=jax.ShapeDtypeStruct((m, n), out_dtype),
        grid=(m // tm, n // tn),
        in_specs=[
            pl.BlockSpec((tm, d), lambda i, j: (i, 0)),
            pl.BlockSpec((1, d), lambda i, j: (0, 0)),
            pl.BlockSpec((d, tn), lambda i, j: (0, j)),
            pl.BlockSpec((1, tn), lambda i, j: (0, j)),
        ],
        out_specs=pl.BlockSpec((tm, tn), lambda i, j: (i, j)),
        scratch_shapes=[pltpu.VMEM((tm, d), BF16)],
        compiler_params=_params("parallel", "arbitrary"),
        name="sb_query",
    )(x, gain.reshape(1, d), w_q, q_norm.reshape(1, n))


def _suffix_matrix(tk):
    j = lax.broadcasted_iota(jnp.int32, (tk, tk + LANES), 0)
    s = lax.broadcasted_iota(jnp.int32, (tk, tk + LANES), 1)
    return jnp.where((j > s) | (s >= tk), 1.0, 0.0).astype(BF16)


def _split_bf16(x):
    hi = x.astype(BF16)
    return hi, (x - hi.astype(F32)).astype(BF16)


def _sb_weights(qks, biases, carries, suffix, masks, chained):
    tq, tk = qks[0].shape
    log_betas, parts = [], []
    for qk, bias, mask in zip(qks, biases, masks):
        z = qk * (SB_HD ** -0.5) + bias
        t = jnp.log(1.0 + jnp.exp(-jnp.abs(z)))
        log_betas.append(jnp.minimum(z, 0.0) - t)
        log_keep = jnp.minimum(-z, 0.0) - t
        if mask is not None:
            log_keep = jnp.where(mask, log_keep, 0.0)
        parts.extend(_split_bf16(log_keep))
    r_all = _dot(jnp.concatenate(parts, axis=0), suffix)
    weights, new_carries = [], []
    carry = carries if chained else None
    for i, (log_beta, mask) in enumerate(zip(log_betas, masks)):
        r = r_all[2 * i * tq:(2 * i + 1) * tq] + r_all[(2 * i + 1) * tq:(2 * i + 2) * tq]
        if not chained:
            carry = carries[i]
        carry_w = carry if tk == LANES else jnp.concatenate([carry] * (tk // LANES), axis=1)
        a = jnp.exp(log_beta + r[:, :tk] + carry_w)
        if mask is not None:
            a = jnp.where(mask, a, 0.0)
        weights.append(a.astype(BF16))
        carry = carry + r[:, tk:]
        new_carries.append(carry)
    return weights, (carry if chained else new_carries)


def _sb_prompt_kernel(bias_ref, q_ref, k_ref, v_ref, o_ref, acc_ref, carry_ref, *, tq, heads):
    hg = pl.program_id(1)
    i = pl.program_id(2)
    suffix = _suffix_matrix(tq)
    row = lax.broadcasted_iota(jnp.int32, (tq, tq), 0)
    col = lax.broadcasted_iota(jnp.int32, (tq, tq), 1)
    diag_mask = col < row

    def tile(j, mask, first):
        start = pl.multiple_of(j * tq, tq)
        cols = [slice(hh * SB_HD, (hh + 1) * SB_HD) for hh in range(heads)]
        qks = [_dot_nt(q_ref[:, cs], k_ref[pl.ds(start, tq), cs]) for cs in cols]
        biases = [bias_ref[hg * heads + hh] for hh in range(heads)]
        carries = [jnp.zeros((tq, LANES), F32) if first else carry_ref[hh] for hh in range(heads)]
        weights, carries = _sb_weights(qks, biases, carries, suffix, [mask] * heads, chained=False)
        for hh, cs in enumerate(cols):
            av = _dot(weights[hh], v_ref[pl.ds(start, tq), cs])
            carry_ref[hh] = carries[hh]
            if first:
                acc_ref[hh] = av
            else:
                acc_ref[hh] += av

    tile(i, diag_mask, True)

    def body(t, _):
        tile(i - 1 - t, None, False)
        return 0

    lax.fori_loop(0, i, body, 0)
    for hh in range(heads):
        o_ref[:, hh * SB_HD:(hh + 1) * SB_HD] = acc_ref[hh].astype(o_ref.dtype)


def _sb_prompt(q, k, v, bias, *, batch, seq, tq, heads):
    m, n = q.shape
    nq = seq // tq
    w = heads * SB_HD
    assert seq % tq == 0 and n % w == 0 and tq % LANES == 0
    kern = functools.partial(_sb_prompt_kernel, tq=tq, heads=heads)
    return pl.pallas_call(
        kern,
        out_shape=jax.ShapeDtypeStruct((m, n), BF16),
        grid=(batch, n // w, nq),
        in_specs=[
            pl.BlockSpec(memory_space=pltpu.SMEM),
            pl.BlockSpec((tq, w), lambda b, h, i: (b * nq + i, h)),
            pl.BlockSpec((seq, w), lambda b, h, i: (b, h)),
            pl.BlockSpec((seq, w), lambda b, h, i: (b, h)),
        ],
        out_specs=pl.BlockSpec((tq, w), lambda b, h, i: (b * nq + i, h)),
        scratch_shapes=[pltpu.VMEM((heads, tq, SB_HD), F32), pltpu.VMEM((heads, tq, LANES), F32)],
        compiler_params=_params("parallel", "parallel", "arbitrary"),
        name="sb_prompt",
    )(bias, q, k, v)


def _sb_sample_kernel(pt_ref, q_ref, kn_ref, vn_ref, bias_ref, own_ref, fold_ref, spread_ref, *refs,
                      c, pages_per_step, page):
    k_refs = refs[:pages_per_step]
    v_refs = refs[pages_per_step:2 * pages_per_step]
    o_ref = refs[2 * pages_per_step]
    kpad_ref, vpad_ref, acc_ref, carry_ref = refs[2 * pages_per_step + 1:]
    jj = pl.program_id(1)
    rows = SB_HEADS * c
    suffix = _suffix_matrix(page)
    bias = bias_ref[...]
    own = own_ref[...]
    q = q_ref[0].astype(BF16)

    def tiles(pks, pvs, carry, masks):
        n = len(pks)
        parts = []
        for pk in pks:
            parts.extend(_split_bf16(_dot_nt(q, pk) * own))
        folded = _dot(jnp.concatenate(parts, axis=0), fold_ref[...])
        qks = [folded[2 * i * rows:(2 * i + 1) * rows] + folded[(2 * i + 1) * rows:(2 * i + 2) * rows]
               for i in range(n)]
        weights, carry = _sb_weights(qks, [bias] * n, carry, suffix, masks, chained=True)
        spread = _dot(jnp.concatenate(weights, axis=0), spread_ref[...])
        av = None
        for i, pv in enumerate(pvs):
            a_wide = (spread[i * rows:(i + 1) * rows] * own).astype(BF16)
            term = _dot(a_wide, pv)
            av = term if av is None else av + term
        return av, carry

    @pl.when(jj == 0)
    def _():
        kpad_ref[...] = jnp.zeros_like(kpad_ref)
        vpad_ref[...] = jnp.zeros_like(vpad_ref)
        kpad_ref[0:c * SB_HEADS, :] = kn_ref[0]
        vpad_ref[0:c * SB_HEADS, :] = vn_ref[0]
        r_idx = lax.broadcasted_iota(jnp.int32, (rows, page), 0)
        s_idx = lax.broadcasted_iota(jnp.int32, (rows, page), 1)
        av, carry = tiles([kpad_ref[...].astype(BF16)], [vpad_ref[...].astype(BF16)],
                          jnp.zeros((rows, LANES), F32), [s_idx < (r_idx % c)])
        acc_ref[...] = av
        carry_ref[...] = carry

    av, carry = tiles([r[0].astype(BF16) for r in k_refs], [r[0].astype(BF16) for r in v_refs],
                      carry_ref[...], [None] * pages_per_step)
    carry_ref[...] = carry
    acc_ref[...] += av

    @pl.when(jj == pl.num_programs(1) - 1)
    def _():
        o_ref[0] = acc_ref[...]


def _sb_sample(q, k_new, v_new, cache_k, cache_v, page_table, bias, *, pages_per_step):
    batch, rows, hd = q.shape
    c = rows // SB_HEADS
    n_pages = page_table.shape[1]
    prow = cache_k.shape[1]
    page = prow // SB_HEADS
    steps = n_pages // pages_per_step
    assert n_pages % pages_per_step == 0 and page % LANES == 0
    bias_rows = jnp.broadcast_to(jnp.repeat(bias.astype(F32), c)[:, None], (rows, page))
    r_head = jnp.arange(rows)[:, None] // c
    col = jnp.arange(prow)[None, :]
    own = (col % SB_HEADS == r_head).astype(F32)
    fold = (jnp.arange(prow)[:, None] // SB_HEADS == jnp.arange(page)[None, :]).astype(BF16)
    spread = fold.T

    def page_spec(p):
        return pl.BlockSpec((1, prow, hd),
                            lambda b, jj, pt: (pt[b * n_pages + n_pages - 1 - (jj * pages_per_step + p)], 0, 0))

    const = lambda shape: pl.BlockSpec(shape, lambda b, jj, pt: (0,) * len(shape))
    kern = functools.partial(_sb_sample_kernel, c=c, pages_per_step=pages_per_step, page=page)
    grid_spec = pltpu.PrefetchScalarGridSpec(
        num_scalar_prefetch=1,
        grid=(batch, steps),
        in_specs=[pl.BlockSpec((1, rows, hd), lambda b, jj, pt: (b, 0, 0)),
                  pl.BlockSpec((1, c * SB_HEADS, hd), lambda b, jj, pt: (b, 0, 0)),
                  pl.BlockSpec((1, c * SB_HEADS, hd), lambda b, jj, pt: (b, 0, 0)),
                  const((rows, page)), const((rows, prow)), const((prow, page)), const((page, prow))]
        + [page_spec(p) for p in range(pages_per_step)] * 2,
        out_specs=pl.BlockSpec((1, rows, hd), lambda b, jj, pt: (b, 0, 0)),
        scratch_shapes=[pltpu.VMEM((prow, hd), F32), pltpu.VMEM((prow, hd), F32),
                        pltpu.VMEM((rows, hd), F32), pltpu.VMEM((rows, LANES), F32)],
    )
    return pl.pallas_call(
        kern,
        out_shape=jax.ShapeDtypeStruct((batch, rows, hd), F32),
        grid_spec=grid_spec,
        compiler_params=_params("parallel", "arbitrary"),
        name="sb_sample",
    )(page_table.reshape(-1), q, k_new, v_new, bias_rows, own, fold, spread,
      *([cache_k] * pages_per_step), *([cache_v] * pages_per_step))


def _rope_tables(pos, half):
    inv = ROPE_BASE ** (-jnp.arange(half, dtype=F32) / half)
    ang = pos[:, None] * inv[None, :]
    return jnp.cos(ang), jnp.sin(ang)


def kernel(x_prompt, x_sample, state_ret, cache_k, cache_v, page_table, ffn1_norm, ffn1_w_in, ffn1_w_out, mix_norm, ffn2_norm, ffn2_w_in, ffn2_w_out, ret_w_in, ret_gn, ret_w_out, kv_norm, w_kv, k_norm, sb_w_q, sb_q_norm, sb_bias, sb_w_out):
    batch, seq, d = x_prompt.shape
    dec_b, dec_s, _ = x_sample.shape
    n_pool, page = cache_k.shape[:2]
    past_len = page_table.shape[1] * page
    depth = ffn1_norm.shape[0]
    n_a = ret_w_in.shape[0]
    dk = d // RET_HEADS
    mp, ms = batch * seq, dec_b * dec_s

    bf = lambda w: w.astype(BF16)
    w_kv = bf(w_kv)

    tm_p, tm_s = 512, ms
    tm_big = min(1024, mp)
    cos_p, sin_p = _rope_tables(jnp.arange(seq, dtype=F32), dk // 2)
    cos_s, sin_s = _rope_tables(past_len + jnp.arange(dec_s, dtype=F32), dk // 2)
    cos_s, sin_s = jnp.tile(cos_s, (dec_b, 1)), jnp.tile(sin_s, (dec_b, 1))

    xp = x_prompt.reshape(mp, d)
    xs = x_sample.reshape(ms, d)
    cache_k2 = cache_k.reshape(n_pool, page * SB_HEADS, SB_HD)
    cache_v2 = cache_v.reshape(n_pool, page * SB_HEADS, SB_HD)

    def ffn_pair(xp, xs, norm, w_in, w_out):
        return (_ffn(xp, norm, w_in, w_out, tm=tm_p, tf=512),
                _ffn(xs, norm, w_in, w_out, tm=tm_s, tf=512))

    ret_p, ret_s = [], []
    k_p = v_p = k_s = v_s = kb_p = vb_p = None
    for layer in range(depth):
        if layer == n_a:
            k_p, v_p, kb_p, vb_p = _shared_kv(xp, kv_norm, w_kv, k_norm, tm=tm_big, tn=512)
            k_s, v_s, _, _ = _shared_kv(xs, kv_norm, w_kv, k_norm, tm=tm_s, tn=512)

        xp, xs = ffn_pair(xp, xs, ffn1_norm[layer], bf(ffn1_w_in[layer]), bf(ffn1_w_out[layer]))

        if layer < n_a:
            a = layer
            w_in, w_out = bf(ret_w_in[a]), bf(ret_w_out[a])
            proj_p = _ret_in(xp, mix_norm[layer], w_in, cos_p, sin_p, tm=tm_big, tn=1024, out_dtype=BF16)
            proj_s = _ret_in(xs, mix_norm[layer], w_in, cos_s, sin_s, tm=tm_s, tn=1024, out_dtype=F32)
            og_p, st_p = _ret_prompt(proj_p, ret_gn[a], batch=batch, seq=seq)
            og_s, st_s = _ret_sample(proj_s.reshape(dec_b, dec_s, -1), state_ret[a], ret_gn[a])
            ret_p.append(st_p)
            ret_s.append(st_s)
            xp = _matmul_res(og_p, w_out, xp, tm=tm_big, tn=512)
            xs = _matmul_res(og_s.reshape(ms, -1), w_out, xs, tm=tm_s, tn=512)
        else:
            bl = layer - n_a
            w_q, w_out = bf(sb_w_q[bl]), bf(sb_w_out[bl])
            q_p = _sb_query(xp, mix_norm[layer], w_q, sb_q_norm[bl], tm=tm_big, tn=1024, out_dtype=BF16)
            q_s = _sb_query(xs, mix_norm[layer], w_q, sb_q_norm[bl], tm=tm_s, tn=1024, out_dtype=F32)
            at_p = _sb_prompt(q_p, kb_p, vb_p, sb_bias[bl], batch=batch, seq=seq, tq=256, heads=4)
            q_hq = q_s.reshape(dec_b, dec_s, SB_HEADS, SB_HD).transpose(0, 2, 1, 3)
            at_hq = _sb_sample(q_hq.reshape(dec_b, SB_HEADS * dec_s, SB_HD),
                               k_s.reshape(dec_b, dec_s * SB_HEADS, SB_HD),
                               v_s.reshape(dec_b, dec_s * SB_HEADS, SB_HD),
                               cache_k2, cache_v2, page_table, sb_bias[bl], pages_per_step=8)
            at_s = at_hq.reshape(dec_b, SB_HEADS, dec_s, SB_HD).transpose(0, 2, 1, 3).reshape(ms, d)
            xp = _matmul_res(at_p, w_out, xp, tm=tm_big, tn=1024)
            xs = _matmul_res(at_s, w_out, xs, tm=tm_s, tn=1024)

        xp, xs = ffn_pair(xp, xs, ffn2_norm[layer], bf(ffn2_w_in[layer]), bf(ffn2_w_out[layer]))

    hd_shape = (SB_HEADS, SB_HD)
    return (xp.reshape(batch, seq, d), xs.reshape(dec_b, dec_s, d),
            jnp.stack(ret_p, axis=0), jnp.stack(ret_s, axis=0),
            k_p.reshape(batch, seq, *hd_shape), v_p.reshape(batch, seq, *hd_shape),
            k_s.reshape(dec_b, dec_s, *hd_shape), v_s.reshape(dec_b, dec_s, *hd_shape))
```

```python
import functools
import math

import jax
import jax.numpy as jnp
from jax import lax
from jax.experimental import pallas as pl
from jax.experimental.pallas import tpu as pltpu

F32 = jnp.float32
BF16 = jnp.bfloat16

EPS = 1e-6
FFN_RES = 0.5
ROPE_BASE = 10000.0
RET_HEADS = 8
RET_CHUNK = 128
SB_HEADS = 16
SB_HD = 128

LANES = 128
SUBLANES = 8
VMEM_LIMIT_BYTES = 56 * 1024 * 1024

SB_HEAD_GROUP = SUBLANES


def _params(*semantics):
    return pltpu.CompilerParams(dimension_semantics=semantics, vmem_limit_bytes=VMEM_LIMIT_BYTES)


def _rmsnorm_rows(x, gain):
    return x * lax.rsqrt(jnp.mean(x * x, axis=-1, keepdims=True) + EPS) * gain


def _head_rmsnorm(y, gain, hd):
    outs = []
    for c in range(y.shape[1] // hd):
        blk = y[:, c * hd:(c + 1) * hd]
        outs.append(blk * lax.rsqrt(jnp.mean(blk * blk, axis=-1, keepdims=True) + EPS)
                    * gain[:, c * hd:(c + 1) * hd])
    return outs


def _dot(a, b):
    return jnp.dot(a, b, preferred_element_type=F32)


def _dot_nt(a, b):
    return lax.dot_general(a, b, (((1,), (1,)), ((), ())), preferred_element_type=F32)


def _dot_tn(a, b):
    return lax.dot_general(a, b, (((0,), (0,)), ((), ())), preferred_element_type=F32)


def _ffn_kernel(x_ref, g_ref, wg_ref, wu_ref, wo_ref, o_ref, h_ref):
    j = pl.program_id(1)

    @pl.when(j == 0)
    def _():
        x = x_ref[...]
        h_ref[...] = _rmsnorm_rows(x, g_ref[...]).astype(BF16)
        o_ref[...] = x

    h = h_ref[...]
    gate = _dot(h, wg_ref[...])
    up = _dot(h, wu_ref[...])
    act = (gate * jax.nn.sigmoid(gate) * up).astype(BF16)
    o_ref[...] += _dot(act, wo_ref[...])


def _ffn(x, gain, w_in, w_out_scaled, layer, *, tm, tf):
    m, d = x.shape
    dff = w_out_scaled.shape[1]
    nj = dff // tf
    assert m % tm == 0 and dff % tf == 0
    return pl.pallas_call(
        _ffn_kernel,
        out_shape=jax.ShapeDtypeStruct((m, d), F32),
        grid=(m // tm, nj),
        in_specs=[
            pl.BlockSpec((tm, d), lambda i, j: (i, 0)),
            pl.BlockSpec((1, d), lambda i, j: (0, 0)),
            pl.BlockSpec((None, d, tf), lambda i, j: (layer, 0, j)),
            pl.BlockSpec((None, d, tf), lambda i, j: (layer, 0, j + nj)),
            pl.BlockSpec((None, tf, d), lambda i, j: (layer, j, 0)),
        ],
        out_specs=pl.BlockSpec((tm, d), lambda i, j: (i, 0)),
        scratch_shapes=[pltpu.VMEM((tm, d), BF16)],
        compiler_params=_params("parallel", "arbitrary"),
        name="ffn",
    )(x, gain.reshape(1, d), w_in, w_in, w_out_scaled)


def _ret_in_kernel(x_ref, g_ref, w_ref, cos_ref, sin_ref, o_ref, h_ref, *, nq, nk, dk, k_scale):
    j = pl.program_id(1)

    @pl.when(j == 0)
    def _():
        h_ref[...] = _rmsnorm_rows(x_ref[...], g_ref[...]).astype(BF16)

    y = _dot(h_ref[...], w_ref[...])
    half = dk // 2

    @pl.when(j < nq + nk)
    def _():
        sc = jnp.where(j < nq, 1.0, k_scale).astype(F32)
        cos = cos_ref[...]
        sin = sin_ref[...]
        for hh in range(y.shape[1] // dk):
            x1 = y[:, hh * dk:hh * dk + half]
            x2 = y[:, hh * dk + half:(hh + 1) * dk]
            o_ref[:, hh * dk:hh * dk + half] = ((x1 * cos - x2 * sin) * sc).astype(o_ref.dtype)
            o_ref[:, hh * dk + half:(hh + 1) * dk] = ((x1 * sin + x2 * cos) * sc).astype(o_ref.dtype)

    @pl.when(j >= nq + nk)
    def _():
        o_ref[...] = y.astype(o_ref.dtype)


def _ret_in(x, gain, w, cos, sin, *, tm, tn, out_dtype):
    m, d = x.shape
    n = w.shape[1]
    dk = d // RET_HEADS
    nper = cos.shape[0] // tm
    assert m % tm == 0 and n % tn == 0 and tn % dk == 0 and cos.shape[0] % tm == 0
    kern = functools.partial(_ret_in_kernel, nq=d // tn, nk=d // tn, dk=dk, k_scale=dk ** -0.5)
    return pl.pallas_call(
        kern,
        out_shape=jax.ShapeDtypeStruct((m, n), out_dtype),
        grid=(m // tm, n // tn),
        in_specs=[
            pl.BlockSpec((tm, d), lambda i, j: (i, 0)),
            pl.BlockSpec((1, d), lambda i, j: (0, 0)),
            pl.BlockSpec((d, tn), lambda i, j: (0, j)),
            pl.BlockSpec((tm, dk // 2), lambda i, j: (i % nper, 0)),
            pl.BlockSpec((tm, dk // 2), lambda i, j: (i % nper, 0)),
        ],
        out_specs=pl.BlockSpec((tm, tn), lambda i, j: (i, j)),
        scratch_shapes=[pltpu.VMEM((tm, d), BF16)],
        compiler_params=_params("parallel", "arbitrary"),
        name="ret_in",
    )(x, gain.reshape(1, d), w, cos, sin)


def _retention_heads(q_ref, k_ref, v_ref, g_ref, gn_ref, og_ref, get_state, put_state, *, c, dk, dv):
    row = lax.broadcasted_iota(jnp.int32, (c, c), 0)
    col = lax.broadcasted_iota(jnp.int32, (c, c), 1)
    diff = (row - col).astype(F32)
    idx = lax.broadcasted_iota(jnp.int32, (c, 1), 0).astype(F32)
    for h in range(RET_HEADS):
        lg = math.log(1.0 - 2.0 ** (-5.0 - h))
        q = q_ref[:, h * dk:(h + 1) * dk]
        k = k_ref[:, h * dk:(h + 1) * dk]
        v = v_ref[:, h * dv:(h + 1) * dv].astype(BF16)
        state = get_state(h)
        decay = jnp.where(diff >= 0, jnp.exp(lg * jnp.maximum(diff, 0.0)), 0.0)
        scores = _dot_nt(q.astype(BF16), k.astype(BF16)) * decay
        inner = _dot(scores.astype(BF16), v)
        q_decay = jnp.exp((idx + 1.0) * lg)
        cross = _dot((q.astype(F32) * q_decay).astype(BF16), state.astype(BF16))
        k_decay = jnp.exp((c - 1.0 - idx) * lg)
        kd = (k.astype(F32) * k_decay).astype(BF16)
        put_state(h, math.exp(c * lg) * state + _dot_tn(kd, v))
        o = inner + cross
        on = o * lax.rsqrt(jnp.mean(o * o, axis=-1, keepdims=True) + EPS) * gn_ref[:, h * dv:(h + 1) * dv]
        g = g_ref[:, h * dv:(h + 1) * dv].astype(F32)
        og_ref[:, h * dv:(h + 1) * dv] = (on * (g * jax.nn.sigmoid(g))).astype(og_ref.dtype)


def _ret_prompt_kernel(q_ref, k_ref, v_ref, g_ref, gn_ref, og_ref, st_ref, *, c, dk, dv):
    @pl.when(pl.program_id(1) == 0)
    def _():
        st_ref[...] = jnp.zeros_like(st_ref)

    def get_state(h):
        return st_ref[0, h]

    def put_state(h, s):
        st_ref[0, h] = s

    _retention_heads(q_ref, k_ref, v_ref, g_ref, gn_ref, og_ref, get_state, put_state, c=c, dk=dk, dv=dv)


def _ret_prompt(proj, gn, *, batch, seq):
    m, n = proj.shape
    d = n // 6
    dk, dv, c = d // RET_HEADS, 2 * d // RET_HEADS, RET_CHUNK
    nc = seq // c
    kern = functools.partial(_ret_prompt_kernel, c=c, dk=dk, dv=dv)
    og, st = pl.pallas_call(
        kern,
        out_shape=(jax.ShapeDtypeStruct((m, 2 * d), BF16),
                   jax.ShapeDtypeStruct((batch, RET_HEADS, dk, dv), F32)),
        grid=(batch, nc),
        in_specs=[
            pl.BlockSpec((c, d), lambda b, t: (b * nc + t, 0)),
            pl.BlockSpec((c, d), lambda b, t: (b * nc + t, 1)),
            pl.BlockSpec((c, 2 * d), lambda b, t: (b * nc + t, 1)),
            pl.BlockSpec((c, 2 * d), lambda b, t: (b * nc + t, 2)),
            pl.BlockSpec((1, 2 * d), lambda b, t: (0, 0)),
        ],
        out_specs=(pl.BlockSpec((c, 2 * d), lambda b, t: (b * nc + t, 0)),
                   pl.BlockSpec((1, RET_HEADS, dk, dv), lambda b, t: (b, 0, 0, 0))),
        compiler_params=_params("parallel", "arbitrary"),
        name="ret_prompt",
    )(proj, proj, proj, proj, gn.reshape(1, 2 * d))
    return og, st


def _ret_sample_kernel(q_ref, k_ref, v_ref, g_ref, gn_ref, st_in_ref, og_ref, st_ref, *, c, dk, dv):
    def get_state(h):
        return st_in_ref[0, h]

    def put_state(h, s):
        st_ref[0, h] = s

    _retention_heads(q_ref.at[0], k_ref.at[0], v_ref.at[0], g_ref.at[0], gn_ref, og_ref.at[0],
                     get_state, put_state, c=c, dk=dk, dv=dv)


def _ret_sample(proj, state, gn):
    batch, c, n = proj.shape
    d = n // 6
    dk, dv = d // RET_HEADS, 2 * d // RET_HEADS
    kern = functools.partial(_ret_sample_kernel, c=c, dk=dk, dv=dv)
    og, st = pl.pallas_call(
        kern,
        out_shape=(jax.ShapeDtypeStruct((batch, c, 2 * d), F32),
                   jax.ShapeDtypeStruct((batch, RET_HEADS, dk, dv), F32)),
        grid=(batch,),
        in_specs=[
            pl.BlockSpec((1, c, d), lambda b: (b, 0, 0)),
            pl.BlockSpec((1, c, d), lambda b: (b, 0, 1)),
            pl.BlockSpec((1, c, 2 * d), lambda b: (b, 0, 1)),
            pl.BlockSpec((1, c, 2 * d), lambda b: (b, 0, 2)),
            pl.BlockSpec((1, 2 * d), lambda b: (0, 0)),
            pl.BlockSpec((1, RET_HEADS, dk, dv), lambda b: (b, 0, 0, 0)),
        ],
        out_specs=(pl.BlockSpec((1, c, 2 * d), lambda b: (b, 0, 0)),
                   pl.BlockSpec((1, RET_HEADS, dk, dv), lambda b: (b, 0, 0, 0))),
        compiler_params=_params("parallel"),
        name="ret_sample",
    )(proj, proj, proj, proj, gn.reshape(1, 2 * d), state)
    return og, st


def _matmul_res_kernel(a_ref, w_ref, r_ref, o_ref):
    o_ref[...] = r_ref[...] + _dot(a_ref[...].astype(BF16), w_ref[...])


def _matmul_res(a, w, res, *, tm, tn):
    m, k = a.shape
    n = w.shape[1]
    assert m % tm == 0 and n % tn == 0
    return pl.pallas_call(
        _matmul_res_kernel,
        out_shape=jax.ShapeDtypeStruct((m, n), F32),
        grid=(m // tm, n // tn),
        in_specs=[
            pl.BlockSpec((tm, k), lambda i, j: (i, 0)),
            pl.BlockSpec((k, tn), lambda i, j: (0, j)),
            pl.BlockSpec((tm, tn), lambda i, j: (i, j)),
        ],
        out_specs=pl.BlockSpec((tm, tn), lambda i, j: (i, j)),
        compiler_params=_params("parallel", "arbitrary"),
        name="matmul_res",
    )(a, w, res)


def _kv_kernel(x_ref, g_ref, wk_ref, wv_ref, kn_ref, k_ref, v_ref, kb_ref, vb_ref, h_ref):
    @pl.when(pl.program_id(1) == 0)
    def _():
        h_ref[...] = _rmsnorm_rows(x_ref[...], g_ref[...]).astype(BF16)

    h = h_ref[...]
    yk = _dot(h, wk_ref[...])
    for c, blk in enumerate(_head_rmsnorm(yk, kn_ref[...], SB_HD)):
        k_ref[:, c * SB_HD:(c + 1) * SB_HD] = blk
        kb_ref[:, c * SB_HD:(c + 1) * SB_HD] = blk.astype(BF16)
    yv = _dot(h, wv_ref[...])
    v_ref[...] = yv
    vb_ref[...] = yv.astype(BF16)


def _shared_kv(x, gain, w_kv, k_norm, *, tm, tn):
    m, d = x.shape
    n = w_kv.shape[1] // 2
    nj = n // tn
    assert m % tm == 0 and n % tn == 0 and tn % SB_HD == 0
    blk = pl.BlockSpec((tm, tn), lambda i, j: (i, j))
    return pl.pallas_call(
        _kv_kernel,
        out_shape=(jax.ShapeDtypeStruct((m, n), F32), jax.ShapeDtypeStruct((m, n), F32),
                   jax.ShapeDtypeStruct((m, n), BF16), jax.ShapeDtypeStruct((m, n), BF16)),
        grid=(m // tm, nj),
        in_specs=[
            pl.BlockSpec((tm, d), lambda i, j: (i, 0)),
            pl.BlockSpec((1, d), lambda i, j: (0, 0)),
            pl.BlockSpec((d, tn), lambda i, j: (0, j)),
            pl.BlockSpec((d, tn), lambda i, j: (0, j + nj)),
            pl.BlockSpec((1, tn), lambda i, j: (0, j)),
        ],
        out_specs=(blk, blk, blk, blk),
        scratch_shapes=[pltpu.VMEM((tm, d), BF16)],
        compiler_params=_params("parallel", "arbitrary"),
        name="shared_kv",
    )(x, gain.reshape(1, d), w_kv, w_kv, k_norm.reshape(1, n))


def _sbq_kernel(x_ref, g_ref, w_ref, qn_ref, o_ref, h_ref):
    @pl.when(pl.program_id(1) == 0)
    def _():
        h_ref[...] = _rmsnorm_rows(x_ref[...], g_ref[...]).astype(BF16)

    y = _dot(h_ref[...], w_ref[...])
    for c, blk in enumerate(_head_rmsnorm(y, qn_ref[...], SB_HD)):
        o_ref[:, c * SB_HD:(c + 1) * SB_HD] = blk.astype(o_ref.dtype)


def _sb_query(x, gain, w_q, q_norm, *, tm, tn, out_dtype):
    m, d = x.shape
    n = w_q.shape[1]
    assert m % tm == 0 and n % tn == 0 and tn % SB_HD == 0
    return pl.pallas_call(
        _sbq_kernel,
        out_shape=jax.ShapeDtypeStruct((m, n), out_dtype),
        grid=(m // tm, n // tn),
        in_specs=[
            pl.BlockSpec((tm, d), lambda i, j: (i, 0)),
            pl.BlockSpec((1, d), lambda i, j: (0, 0)),
            pl.BlockSpec((d, tn), lambda i, j: (0, j)),
            pl.BlockSpec((1, tn), lambda i, j: (0, j)),
        ],
        out_specs=pl.BlockSpec((tm, tn), lambda i, j: (i, j)),
        scratch_shapes=[pltpu.VMEM((tm, d), BF16)],
        compiler_params=_params("parallel", "arbitrary"),
        name="sb_query",
    )(x, gain.reshape(1, d), w_q, q_norm.reshape(1, n))


def _suffix_matrix(tk, with_totals):
    width = tk + LANES if with_totals else tk
    j = lax.broadcasted_iota(jnp.int32, (tk, width), 0)
    s = lax.broadcasted_iota(jnp.int32, (tk, width), 1)
    return jnp.where((j > s) | (s >= tk), 1.0, 0.0).astype(BF16)


def _split_bf16(x):
    hi = x.astype(BF16)
    return hi, (x - hi.astype(F32)).astype(BF16)


def _sb_weights(qks, biases, carries, suffix, masks, chained, split):
    tq, tk = qks[0].shape
    log_betas, first_cols, parts = [], [], []
    for qk, bias, mask in zip(qks, biases, masks):
        z = qk * (SB_HD ** -0.5) + bias
        t = jnp.log(1.0 + jnp.exp(-jnp.abs(z)))
        log_betas.append(jnp.minimum(z, 0.0) - t)
        log_keep = jnp.minimum(-z, 0.0) - t
        if mask is not None:
            log_keep = jnp.where(mask, log_keep, 0.0)
        first_cols.append(log_keep[:, 0:1])
        parts.extend(_split_bf16(log_keep) if split else [log_keep.astype(BF16)])
    r_all = _dot(jnp.concatenate(parts, axis=0), suffix)
    n_parts = 2 if split else 1
    weights, new_carries = [], []
    carry = carries if chained else None
    for i, (log_beta, mask) in enumerate(zip(log_betas, masks)):
        r = r_all[n_parts * i * tq:(n_parts * i + 1) * tq]
        if split:
            r = r + r_all[(2 * i + 1) * tq:(2 * i + 2) * tq]
        if not chained:
            carry = carries[i]
        carry_w = carry if tk == LANES else jnp.concatenate([carry] * (tk // LANES), axis=1)
        a = jnp.exp(log_beta + r[:, :tk] + carry_w)
        if mask is not None:
            a = jnp.where(mask, a, 0.0)
        weights.append(a.astype(BF16))
        if suffix.shape[1] > tk:
            tile_total = r[:, tk:]
        else:
            tile_total = jnp.broadcast_to(r[:, 0:1] + first_cols[i], (tq, LANES))
        carry = carry + tile_total
        new_carries.append(carry)
    return weights, (carry if chained else new_carries)


def _sb_prompt_kernel(bias_ref, q_ref, k_ref, v_ref, o_ref, acc_ref, carry_ref, *, tq, heads):
    hg = pl.program_id(1)
    i = pl.program_id(2)
    suffix = _suffix_matrix(tq, with_totals=False)
    row = lax.broadcasted_iota(jnp.int32, (tq, tq), 0)
    col = lax.broadcasted_iota(jnp.int32, (tq, tq), 1)
    diag_mask = col < row

    def tile(j, mask, first):
        start = pl.multiple_of(j * tq, tq)
        cols = [slice(hh * SB_HD, (hh + 1) * SB_HD) for hh in range(heads)]
        qks = [_dot_nt(q_ref[:, cs], k_ref[pl.ds(start, tq), cs]) for cs in cols]
        biases = [bias_ref[hg * heads + hh] for hh in range(heads)]
        carries = [jnp.zeros((tq, LANES), F32) if first else carry_ref[hh] for hh in range(heads)]
        weights, carries = _sb_weights(qks, biases, carries, suffix, [mask] * heads, chained=False, split=False)
        for hh, cs in enumerate(cols):
            av = _dot(weights[hh], v_ref[pl.ds(start, tq), cs])
            carry_ref[hh] = carries[hh]
            if first:
                acc_ref[hh] = av
            else:
                acc_ref[hh] += av

    tile(i, diag_mask, True)

    def body(t, _):
        tile(i - 1 - t, None, False)
        return 0

    lax.fori_loop(0, i, body, 0)
    for hh in range(heads):
        o_ref[:, hh * SB_HD:(hh + 1) * SB_HD] = acc_ref[hh].astype(o_ref.dtype)


def _sb_prompt(q, k, v, bias, *, batch, seq, tq, heads):
    m, n = q.shape
    nq = seq // tq
    w = heads * SB_HD
    assert seq % tq == 0 and n % w == 0 and tq % LANES == 0
    kern = functools.partial(_sb_prompt_kernel, tq=tq, heads=heads)
    return pl.pallas_call(
        kern,
        out_shape=jax.ShapeDtypeStruct((m, n), BF16),
        grid=(batch, n // w, nq),
        in_specs=[
            pl.BlockSpec(memory_space=pltpu.SMEM),
            pl.BlockSpec((tq, w), lambda b, h, i: (b * nq + i, h)),
            pl.BlockSpec((seq, w), lambda b, h, i: (b, h)),
            pl.BlockSpec((seq, w), lambda b, h, i: (b, h)),
        ],
        out_specs=pl.BlockSpec((tq, w), lambda b, h, i: (b * nq + i, h)),
        scratch_shapes=[pltpu.VMEM((heads, tq, SB_HD), F32), pltpu.VMEM((heads, tq, LANES), F32)],
        compiler_params=_params("parallel", "parallel", "arbitrary"),
        name="sb_prompt",
    )(bias, q, k, v)


def _sb_sample_kernel(pt_ref, q_ref, kn_ref, vn_ref, bias_ref, own_ref, fold_ref, spread_ref, *refs,
                      c, pages_per_step, page):
    k_refs = refs[:pages_per_step]
    v_refs = refs[pages_per_step:2 * pages_per_step]
    o_ref = refs[2 * pages_per_step]
    kpad_ref, vpad_ref, acc_ref, carry_ref = refs[2 * pages_per_step + 1:]
    jj = pl.program_id(1)
    groups = SB_HEADS // SB_HEAD_GROUP
    rows = SB_HEADS * c
    grows = SB_HEAD_GROUP * c
    prow = page * SB_HEAD_GROUP
    suffix = _suffix_matrix(page, with_totals=True)
    bias = bias_ref[...]
    own = own_ref[...]
    q = q_ref[0].astype(BF16)
    q_groups = [q[g * grows:(g + 1) * grows] for g in range(groups)]

    def page_groups(ref):
        return [ref[:, g].reshape(prow, SB_HD).astype(BF16) for g in range(groups)]

    def tiles(pks, pvs, carry, masks):
        n = len(pks)
        parts = []
        for pk in pks:
            for g in range(groups):
                parts.extend(_split_bf16(_dot_nt(q_groups[g], pk[g]) * own))
        folded = _dot(jnp.concatenate(parts, axis=0), fold_ref[...])
        qks = []
        for i in range(n):
            halves = []
            for g in range(groups):
                base = (i * groups + g) * 2 * grows
                halves.append(folded[base:base + grows] + folded[base + grows:base + 2 * grows])
            qks.append(jnp.concatenate(halves, axis=0))
        weights, carry = _sb_weights(qks, [bias] * n, carry, suffix, masks, chained=True, split=True)
        spread = _dot(jnp.concatenate(weights, axis=0), spread_ref[...])
        av = None
        for i, pv in enumerate(pvs):
            outs = []
            for g in range(groups):
                base = i * rows + g * grows
                a_wide = (spread[base:base + grows] * own).astype(BF16)
                outs.append(_dot(a_wide, pv[g]))
            term = jnp.concatenate(outs, axis=0)
            av = term if av is None else av + term
        return av, carry

    @pl.when(jj == 0)
    def _():
        kpad_ref[...] = jnp.zeros_like(kpad_ref)
        vpad_ref[...] = jnp.zeros_like(vpad_ref)
        kpad_ref[0:c] = kn_ref[0]
        vpad_ref[0:c] = vn_ref[0]
        r_idx = lax.broadcasted_iota(jnp.int32, (rows, page), 0)
        s_idx = lax.broadcasted_iota(jnp.int32, (rows, page), 1)
        av, carry = tiles([page_groups(kpad_ref)], [page_groups(vpad_ref)],
                          jnp.zeros((rows, LANES), F32), [s_idx < (r_idx % c)])
        acc_ref[...] = av
        carry_ref[...] = carry

    av, carry = tiles([page_groups(r.at[0]) for r in k_refs], [page_groups(r.at[0]) for r in v_refs],
                      carry_ref[...], [None] * pages_per_step)
    carry_ref[...] = carry
    acc_ref[...] += av

    @pl.when(jj == pl.num_programs(1) - 1)
    def _():
        o_ref[0] = acc_ref[...]


def _sb_sample(q, k_new, v_new, cache_k, cache_v, page_table, bias, *, pages_per_step):
    batch, rows, hd = q.shape
    c = rows // SB_HEADS
    n_pages = page_table.shape[1]
    page, groups, gh = cache_k.shape[1:4]
    prow = page * gh
    grows = gh * c
    steps = n_pages // pages_per_step
    assert n_pages % pages_per_step == 0 and page % LANES == 0 and gh == SB_HEAD_GROUP
    bias_rows = jnp.broadcast_to(jnp.repeat(bias.astype(F32), c)[:, None], (rows, page))
    own = (jnp.arange(prow)[None, :] % gh == jnp.arange(grows)[:, None] // c).astype(F32)
    fold = (jnp.arange(prow)[:, None] // gh == jnp.arange(page)[None, :]).astype(BF16)
    spread = fold.T

    def page_spec(p):
        return pl.BlockSpec(
            (1, page, groups, gh, hd),
            lambda b, jj, pt: (pt[b * n_pages + n_pages - 1 - (jj * pages_per_step + p)], 0, 0, 0, 0))

    const = lambda shape: pl.BlockSpec(shape, lambda b, jj, pt: (0,) * len(shape))
    new_spec = pl.BlockSpec((1, c, groups, gh, hd), lambda b, jj, pt: (b, 0, 0, 0, 0))
    kern = functools.partial(_sb_sample_kernel, c=c, pages_per_step=pages_per_step, page=page)
    grid_spec = pltpu.PrefetchScalarGridSpec(
        num_scalar_prefetch=1,
        grid=(batch, steps),
        in_specs=[pl.BlockSpec((1, rows, hd), lambda b, jj, pt: (b, 0, 0)), new_spec, new_spec,
                  const((rows, page)), const((grows, prow)), const((prow, page)), const((page, prow))]
        + [page_spec(p) for p in range(pages_per_step)] * 2,
        out_specs=pl.BlockSpec((1, rows, hd), lambda b, jj, pt: (b, 0, 0)),
        scratch_shapes=[pltpu.VMEM((page, groups, gh, hd), F32), pltpu.VMEM((page, groups, gh, hd), F32),
                        pltpu.VMEM((rows, hd), F32), pltpu.VMEM((rows, LANES), F32)],
    )
    return pl.pallas_call(
        kern,
        out_shape=jax.ShapeDtypeStruct((batch, rows, hd), F32),
        grid_spec=grid_spec,
        compiler_params=_params("parallel", "arbitrary"),
        name="sb_sample",
    )(page_table.reshape(-1), q, k_new, v_new, bias_rows, own, fold, spread,
      *([cache_k] * pages_per_step), *([cache_v] * pages_per_step))


def _rope_tables(pos, half):
    inv = ROPE_BASE ** (-jnp.arange(half, dtype=F32) / half)
    ang = pos[:, None] * inv[None, :]
    return jnp.cos(ang), jnp.sin(ang)


def kernel(x_prompt, x_sample, state_ret, cache_k, cache_v, page_table, ffn1_norm, ffn1_w_in, ffn1_w_out, mix_norm, ffn2_norm, ffn2_w_in, ffn2_w_out, ret_w_in, ret_gn, ret_w_out, kv_norm, w_kv, k_norm, sb_w_q, sb_q_norm, sb_bias, sb_w_out):
    batch, seq, d = x_prompt.shape
    dec_b, dec_s, _ = x_sample.shape
    n_pool, page = cache_k.shape[:2]
    past_len = page_table.shape[1] * page
    depth = ffn1_norm.shape[0]
    n_a = ret_w_in.shape[0]
    dk = d // RET_HEADS
    mp, ms = batch * seq, dec_b * dec_s

    bf = lambda w: w.astype(BF16)
    w_kv = bf(w_kv)
    ffn_w = ((bf(ffn1_w_in), bf(ffn1_w_out * FFN_RES)), (bf(ffn2_w_in), bf(ffn2_w_out * FFN_RES)))

    tm_p, tm_s = 512, ms
    tm_big = min(1024, mp)
    cos_p, sin_p = _rope_tables(jnp.arange(seq, dtype=F32), dk // 2)
    cos_s, sin_s = _rope_tables(past_len + jnp.arange(dec_s, dtype=F32), dk // 2)
    cos_s, sin_s = jnp.tile(cos_s, (dec_b, 1)), jnp.tile(sin_s, (dec_b, 1))

    xp = x_prompt.reshape(mp, d)
    xs = x_sample.reshape(ms, d)
    grouped = (SB_HEADS // SB_HEAD_GROUP, SB_HEAD_GROUP, SB_HD)
    cache_k2 = cache_k.reshape(n_pool, page, *grouped)
    cache_v2 = cache_v.reshape(n_pool, page, *grouped)

    def ffn_pair(xp, xs, norm, which, layer):
        w_in, w_out = ffn_w[which]
        return (_ffn(xp, norm, w_in, w_out, layer, tm=tm_p, tf=512),
                _ffn(xs, norm, w_in, w_out, layer, tm=tm_s, tf=512))

    ret_p, ret_s = [], []
    k_p = v_p = k_s = v_s = kb_p = vb_p = None
    for layer in range(depth):
        if layer == n_a:
            k_p, v_p, kb_p, vb_p = _shared_kv(xp, kv_norm, w_kv, k_norm, tm=tm_big, tn=512)
            k_s, v_s, _, _ = _shared_kv(xs, kv_norm, w_kv, k_norm, tm=tm_s, tn=512)

        xp, xs = ffn_pair(xp, xs, ffn1_norm[layer], 0, layer)

        if layer < n_a:
            a = layer
            w_in, w_out = bf(ret_w_in[a]), bf(ret_w_out[a])
            proj_p = _ret_in(xp, mix_norm[layer], w_in, cos_p, sin_p, tm=tm_big, tn=1024, out_dtype=BF16)
            proj_s = _ret_in(xs, mix_norm[layer], w_in, cos_s, sin_s, tm=tm_s, tn=1024, out_dtype=F32)
            og_p, st_p = _ret_prompt(proj_p, ret_gn[a], batch=batch, seq=seq)
            og_s, st_s = _ret_sample(proj_s.reshape(dec_b, dec_s, -1), state_ret[a], ret_gn[a])
            ret_p.append(st_p)
            ret_s.append(st_s)
            xp = _matmul_res(og_p, w_out, xp, tm=tm_big, tn=512)
            xs = _matmul_res(og_s.reshape(ms, -1), w_out, xs, tm=tm_s, tn=512)
        else:
            bl = layer - n_a
            w_q, w_out = bf(sb_w_q[bl]), bf(sb_w_out[bl])
            q_p = _sb_query(xp, mix_norm[layer], w_q, sb_q_norm[bl], tm=tm_big, tn=1024, out_dtype=BF16)
            q_s = _sb_query(xs, mix_norm[layer], w_q, sb_q_norm[bl], tm=tm_s, tn=1024, out_dtype=F32)
            at_p = _sb_prompt(q_p, kb_p, vb_p, sb_bias[bl], batch=batch, seq=seq, tq=256, heads=4)
            q_hq = q_s.reshape(dec_b, dec_s, SB_HEADS, SB_HD).transpose(0, 2, 1, 3)
            at_hq = _sb_sample(q_hq.reshape(dec_b, SB_HEADS * dec_s, SB_HD),
                               k_s.reshape(dec_b, dec_s, *grouped), v_s.reshape(dec_b, dec_s, *grouped),
                               cache_k2, cache_v2, page_table, sb_bias[bl], pages_per_step=8)
            at_s = at_hq.reshape(dec_b, SB_HEADS, dec_s, SB_HD).transpose(0, 2, 1, 3).reshape(ms, d)
            xp = _matmul_res(at_p, w_out, xp, tm=tm_big, tn=1024)
            xs = _matmul_res(at_s, w_out, xs, tm=tm_s, tn=1024)

        xp, xs = ffn_pair(xp, xs, ffn2_norm[layer], 1, layer)

    hd_shape = (SB_HEADS, SB_HD)
    return (xp.reshape(batch, seq, d), xs.reshape(dec_b, dec_s, d),
            jnp.stack(ret_p, axis=0), jnp.stack(ret_s, axis=0),
            k_p.reshape(batch, seq, *hd_shape), v_p.reshape(batch, seq, *hd_shape),
            k_s.reshape(dec_b, dec_s, *hd_shape), v_s.reshape(dec_b, dec_s, *hd_shape))
```

```python
import functools
import math

import jax
import jax.numpy as jnp
from jax import lax
from jax.experimental import pallas as pl
from jax.experimental.pallas import tpu as pltpu

F32 = jnp.float32
BF16 = jnp.bfloat16

EPS = 1e-6
LOG2_E = 1.0 / math.log(2.0)
FFN_RES = 0.5
ROPE_BASE = 10000.0
RET_HEADS = 8
RET_CHUNK = 128
SB_HEADS = 16
SB_HD = 128

LANES = 128
SUBLANES = 8
VMEM_LIMIT_BYTES = 56 * 1024 * 1024

SB_HEAD_GROUP = SUBLANES


def _params(*semantics):
    return pltpu.CompilerParams(dimension_semantics=semantics, vmem_limit_bytes=VMEM_LIMIT_BYTES)


def _rmsnorm_rows(x, gain):
    return x * lax.rsqrt(jnp.mean(x * x, axis=-1, keepdims=True) + EPS) * gain


def _head_rmsnorm(y, gain, hd):
    outs = []
    for c in range(y.shape[1] // hd):
        blk = y[:, c * hd:(c + 1) * hd]
        outs.append(blk * lax.rsqrt(jnp.mean(blk * blk, axis=-1, keepdims=True) + EPS)
                    * gain[:, c * hd:(c + 1) * hd])
    return outs


def _dot(a, b):
    return jnp.dot(a, b, preferred_element_type=F32)


def _dot_nt(a, b):
    return lax.dot_general(a, b, (((1,), (1,)), ((), ())), preferred_element_type=F32)


def _dot_tn(a, b):
    return lax.dot_general(a, b, (((0,), (0,)), ((), ())), preferred_element_type=F32)


def _ffn_kernel(x_ref, g_ref, wg_ref, wu_ref, wo_ref, o_ref, h_ref):
    j = pl.program_id(1)

    @pl.when(j == 0)
    def _():
        x = x_ref[...]
        h_ref[...] = _rmsnorm_rows(x, g_ref[...]).astype(BF16)
        o_ref[...] = x

    h = h_ref[...]
    gate = _dot(h, wg_ref[...])
    up = _dot(h, wu_ref[...])
    act = (gate * jax.nn.sigmoid(gate) * up).astype(BF16)
    o_ref[...] += _dot(act, wo_ref[...])


def _ffn(x, gain, w_in, w_out_scaled, layer, *, tm, tf):
    m, d = x.shape
    dff = w_out_scaled.shape[1]
    nj = dff // tf
    assert m % tm == 0 and dff % tf == 0
    return pl.pallas_call(
        _ffn_kernel,
        out_shape=jax.ShapeDtypeStruct((m, d), F32),
        grid=(m // tm, nj),
        in_specs=[
            pl.BlockSpec((tm, d), lambda i, j: (i, 0)),
            pl.BlockSpec((1, d), lambda i, j: (0, 0)),
            pl.BlockSpec((None, d, tf), lambda i, j: (layer, 0, j)),
            pl.BlockSpec((None, d, tf), lambda i, j: (layer, 0, j + nj)),
            pl.BlockSpec((None, tf, d), lambda i, j: (layer, j, 0)),
        ],
        out_specs=pl.BlockSpec((tm, d), lambda i, j: (i, 0)),
        scratch_shapes=[pltpu.VMEM((tm, d), BF16)],
        compiler_params=_params("parallel", "arbitrary"),
        name="ffn",
    )(x, gain.reshape(1, d), w_in, w_in, w_out_scaled)


def _ret_in_kernel(x_ref, g_ref, w_ref, cos_ref, sin_ref, o_ref, h_ref, *, nq, nk, dk, k_scale):
    j = pl.program_id(1)

    @pl.when(j == 0)
    def _():
        h_ref[...] = _rmsnorm_rows(x_ref[...], g_ref[...]).astype(BF16)

    y = _dot(h_ref[...], w_ref[...])
    half = dk // 2

    @pl.when(j < nq + nk)
    def _():
        sc = jnp.where(j < nq, 1.0, k_scale).astype(F32)
        cos = cos_ref[...]
        sin = sin_ref[...]
        for hh in range(y.shape[1] // dk):
            x1 = y[:, hh * dk:hh * dk + half]
            x2 = y[:, hh * dk + half:(hh + 1) * dk]
            o_ref[:, hh * dk:hh * dk + half] = ((x1 * cos - x2 * sin) * sc).astype(o_ref.dtype)
            o_ref[:, hh * dk + half:(hh + 1) * dk] = ((x1 * sin + x2 * cos) * sc).astype(o_ref.dtype)

    @pl.when(j >= nq + nk)
    def _():
        o_ref[...] = y.astype(o_ref.dtype)


def _ret_in(x, gain, w, cos, sin, *, tm, tn, out_dtype):
    m, d = x.shape
    n = w.shape[1]
    dk = d // RET_HEADS
    nper = cos.shape[0] // tm
    assert m % tm == 0 and n % tn == 0 and tn % dk == 0 and cos.shape[0] % tm == 0
    kern = functools.partial(_ret_in_kernel, nq=d // tn, nk=d // tn, dk=dk, k_scale=dk ** -0.5)
    return pl.pallas_call(
        kern,
        out_shape=jax.ShapeDtypeStruct((m, n), out_dtype),
        grid=(m // tm, n // tn),
        in_specs=[
            pl.BlockSpec((tm, d), lambda i, j: (i, 0)),
            pl.BlockSpec((1, d), lambda i, j: (0, 0)),
            pl.BlockSpec((d, tn), lambda i, j: (0, j)),
            pl.BlockSpec((tm, dk // 2), lambda i, j: (i % nper, 0)),
            pl.BlockSpec((tm, dk // 2), lambda i, j: (i % nper, 0)),
        ],
        out_specs=pl.BlockSpec((tm, tn), lambda i, j: (i, j)),
        scratch_shapes=[pltpu.VMEM((tm, d), BF16)],
        compiler_params=_params("parallel", "arbitrary"),
        name="ret_in",
    )(x, gain.reshape(1, d), w, cos, sin)


def _retention_heads(q_ref, k_ref, v_ref, g_ref, gn_ref, og_ref, get_state, put_state, *, c, dk, dv):
    row = lax.broadcasted_iota(jnp.int32, (c, c), 0)
    col = lax.broadcasted_iota(jnp.int32, (c, c), 1)
    diff = (row - col).astype(F32)
    idx = lax.broadcasted_iota(jnp.int32, (c, 1), 0).astype(F32)
    for h in range(RET_HEADS):
        lg = math.log(1.0 - 2.0 ** (-5.0 - h))
        q = q_ref[:, h * dk:(h + 1) * dk]
        k = k_ref[:, h * dk:(h + 1) * dk]
        v = v_ref[:, h * dv:(h + 1) * dv].astype(BF16)
        state = get_state(h)
        decay = jnp.where(diff >= 0, jnp.exp(lg * jnp.maximum(diff, 0.0)), 0.0)
        scores = _dot_nt(q.astype(BF16), k.astype(BF16)) * decay
        inner = _dot(scores.astype(BF16), v)
        q_decay = jnp.exp((idx + 1.0) * lg)
        cross = _dot((q.astype(F32) * q_decay).astype(BF16), state.astype(BF16))
        k_decay = jnp.exp((c - 1.0 - idx) * lg)
        kd = (k.astype(F32) * k_decay).astype(BF16)
        put_state(h, math.exp(c * lg) * state + _dot_tn(kd, v))
        o = inner + cross
        on = o * lax.rsqrt(jnp.mean(o * o, axis=-1, keepdims=True) + EPS) * gn_ref[:, h * dv:(h + 1) * dv]
        g = g_ref[:, h * dv:(h + 1) * dv].astype(F32)
        og_ref[:, h * dv:(h + 1) * dv] = (on * (g * jax.nn.sigmoid(g))).astype(og_ref.dtype)


def _ret_prompt_kernel(q_ref, k_ref, v_ref, g_ref, gn_ref, og_ref, st_ref, *, c, dk, dv):
    @pl.when(pl.program_id(1) == 0)
    def _():
        st_ref[...] = jnp.zeros_like(st_ref)

    def get_state(h):
        return st_ref[0, h]

    def put_state(h, s):
        st_ref[0, h] = s

    _retention_heads(q_ref, k_ref, v_ref, g_ref, gn_ref, og_ref, get_state, put_state, c=c, dk=dk, dv=dv)


def _ret_prompt(proj, gn, *, batch, seq):
    m, n = proj.shape
    d = n // 6
    dk, dv, c = d // RET_HEADS, 2 * d // RET_HEADS, RET_CHUNK
    nc = seq // c
    kern = functools.partial(_ret_prompt_kernel, c=c, dk=dk, dv=dv)
    og, st = pl.pallas_call(
        kern,
        out_shape=(jax.ShapeDtypeStruct((m, 2 * d), BF16),
                   jax.ShapeDtypeStruct((batch, RET_HEADS, dk, dv), F32)),
        grid=(batch, nc),
        in_specs=[
            pl.BlockSpec((c, d), lambda b, t: (b * nc + t, 0)),
            pl.BlockSpec((c, d), lambda b, t: (b * nc + t, 1)),
            pl.BlockSpec((c, 2 * d), lambda b, t: (b * nc + t, 1)),
            pl.BlockSpec((c, 2 * d), lambda b, t: (b * nc + t, 2)),
            pl.BlockSpec((1, 2 * d), lambda b, t: (0, 0)),
        ],
        out_specs=(pl.BlockSpec((c, 2 * d), lambda b, t: (b * nc + t, 0)),
                   pl.BlockSpec((1, RET_HEADS, dk, dv), lambda b, t: (b, 0, 0, 0))),
        compiler_params=_params("parallel", "arbitrary"),
        name="ret_prompt",
    )(proj, proj, proj, proj, gn.reshape(1, 2 * d))
    return og, st


def _ret_sample_kernel(q_ref, k_ref, v_ref, g_ref, gn_ref, st_in_ref, og_ref, st_ref, *, c, dk, dv):
    def get_state(h):
        return st_in_ref[0, h]

    def put_state(h, s):
        st_ref[0, h] = s

    _retention_heads(q_ref.at[0], k_ref.at[0], v_ref.at[0], g_ref.at[0], gn_ref, og_ref.at[0],
                     get_state, put_state, c=c, dk=dk, dv=dv)


def _ret_sample(proj, state, gn):
    batch, c, n = proj.shape
    d = n // 6
    dk, dv = d // RET_HEADS, 2 * d // RET_HEADS
    kern = functools.partial(_ret_sample_kernel, c=c, dk=dk, dv=dv)
    og, st = pl.pallas_call(
        kern,
        out_shape=(jax.ShapeDtypeStruct((batch, c, 2 * d), F32),
                   jax.ShapeDtypeStruct((batch, RET_HEADS, dk, dv), F32)),
        grid=(batch,),
        in_specs=[
            pl.BlockSpec((1, c, d), lambda b: (b, 0, 0)),
            pl.BlockSpec((1, c, d), lambda b: (b, 0, 1)),
            pl.BlockSpec((1, c, 2 * d), lambda b: (b, 0, 1)),
            pl.BlockSpec((1, c, 2 * d), lambda b: (b, 0, 2)),
            pl.BlockSpec((1, 2 * d), lambda b: (0, 0)),
            pl.BlockSpec((1, RET_HEADS, dk, dv), lambda b: (b, 0, 0, 0)),
        ],
        out_specs=(pl.BlockSpec((1, c, 2 * d), lambda b: (b, 0, 0)),
                   pl.BlockSpec((1, RET_HEADS, dk, dv), lambda b: (b, 0, 0, 0))),
        compiler_params=_params("parallel"),
        name="ret_sample",
    )(proj, proj, proj, proj, gn.reshape(1, 2 * d), state)
    return og, st


def _matmul_res_kernel(a_ref, w_ref, r_ref, o_ref):
    o_ref[...] = r_ref[...] + _dot(a_ref[...].astype(BF16), w_ref[...])


def _matmul_res(a, w, res, *, tm, tn):
    m, k = a.shape
    n = w.shape[1]
    assert m % tm == 0 and n % tn == 0
    return pl.pallas_call(
        _matmul_res_kernel,
        out_shape=jax.ShapeDtypeStruct((m, n), F32),
        grid=(m // tm, n // tn),
        in_specs=[
            pl.BlockSpec((tm, k), lambda i, j: (i, 0)),
            pl.BlockSpec((k, tn), lambda i, j: (0, j)),
            pl.BlockSpec((tm, tn), lambda i, j: (i, j)),
        ],
        out_specs=pl.BlockSpec((tm, tn), lambda i, j: (i, j)),
        compiler_params=_params("parallel", "arbitrary"),
        name="matmul_res",
    )(a, w, res)


def _kv_kernel(x_ref, g_ref, wk_ref, wv_ref, kn_ref, k_ref, v_ref, kb_ref, vb_ref, h_ref):
    @pl.when(pl.program_id(1) == 0)
    def _():
        h_ref[...] = _rmsnorm_rows(x_ref[...], g_ref[...]).astype(BF16)

    h = h_ref[...]
    yk = _dot(h, wk_ref[...])
    for c, blk in enumerate(_head_rmsnorm(yk, kn_ref[...], SB_HD)):
        k_ref[:, c * SB_HD:(c + 1) * SB_HD] = blk
        kb_ref[:, c * SB_HD:(c + 1) * SB_HD] = blk.astype(BF16)
    yv = _dot(h, wv_ref[...])
    v_ref[...] = yv
    vb_ref[...] = yv.astype(BF16)


def _shared_kv(x, gain, w_kv, k_norm, *, tm, tn):
    m, d = x.shape
    n = w_kv.shape[1] // 2
    nj = n // tn
    assert m % tm == 0 and n % tn == 0 and tn % SB_HD == 0
    blk = pl.BlockSpec((tm, tn), lambda i, j: (i, j))
    return pl.pallas_call(
        _kv_kernel,
        out_shape=(jax.ShapeDtypeStruct((m, n), F32), jax.ShapeDtypeStruct((m, n), F32),
                   jax.ShapeDtypeStruct((m, n), BF16), jax.ShapeDtypeStruct((m, n), BF16)),
        grid=(m // tm, nj),
        in_specs=[
            pl.BlockSpec((tm, d), lambda i, j: (i, 0)),
            pl.BlockSpec((1, d), lambda i, j: (0, 0)),
            pl.BlockSpec((d, tn), lambda i, j: (0, j)),
            pl.BlockSpec((d, tn), lambda i, j: (0, j + nj)),
            pl.BlockSpec((1, tn), lambda i, j: (0, j)),
        ],
        out_specs=(blk, blk, blk, blk),
        scratch_shapes=[pltpu.VMEM((tm, d), BF16)],
        compiler_params=_params("parallel", "arbitrary"),
        name="shared_kv",
    )(x, gain.reshape(1, d), w_kv, w_kv, k_norm.reshape(1, n))


def _sbq_kernel(x_ref, g_ref, w_ref, qn_ref, o_ref, h_ref):
    @pl.when(pl.program_id(1) == 0)
    def _():
        h_ref[...] = _rmsnorm_rows(x_ref[...], g_ref[...]).astype(BF16)

    y = _dot(h_ref[...], w_ref[...])
    for c, blk in enumerate(_head_rmsnorm(y, qn_ref[...], SB_HD)):
        o_ref[:, c * SB_HD:(c + 1) * SB_HD] = blk.astype(o_ref.dtype)


def _sb_query(x, gain, w_q, q_norm, *, tm, tn, out_dtype):
    m, d = x.shape
    n = w_q.shape[1]
    assert m % tm == 0 and n % tn == 0 and tn % SB_HD == 0
    return pl.pallas_call(
        _sbq_kernel,
        out_shape=jax.ShapeDtypeStruct((m, n), out_dtype),
        grid=(m // tm, n // tn),
        in_specs=[
            pl.BlockSpec((tm, d), lambda i, j: (i, 0)),
            pl.BlockSpec((1, d), lambda i, j: (0, 0)),
            pl.BlockSpec((d, tn), lambda i, j: (0, j)),
            pl.BlockSpec((1, tn), lambda i, j: (0, j)),
        ],
        out_specs=pl.BlockSpec((tm, tn), lambda i, j: (i, j)),
        scratch_shapes=[pltpu.VMEM((tm, d), BF16)],
        compiler_params=_params("parallel", "arbitrary"),
        name="sb_query",
    )(x, gain.reshape(1, d), w_q, q_norm.reshape(1, n))


def _suffix_matrix(tk, with_totals):
    width = tk + LANES if with_totals else tk
    j = lax.broadcasted_iota(jnp.int32, (tk, width), 0)
    s = lax.broadcasted_iota(jnp.int32, (tk, width), 1)
    return jnp.where((j > s) | (s >= tk), 1.0, 0.0).astype(BF16)


def _split_bf16(x):
    hi = x.astype(BF16)
    return hi, (x - hi.astype(F32)).astype(BF16)


def _sb_weights(qks, biases, carries, suffix, masks, chained, split):
    tq, tk = qks[0].shape
    log_betas, first_cols, parts = [], [], []
    for qk, bias, mask in zip(qks, biases, masks):
        z = qk * (SB_HD ** -0.5) + bias
        t = jnp.log(1.0 + jnp.exp2(jnp.abs(z) * -LOG2_E))
        log_beta = jnp.minimum(z, 0.0) - t
        log_betas.append(log_beta)
        log_keep = log_beta - z
        if mask is not None:
            log_keep = jnp.where(mask, log_keep, 0.0)
        first_cols.append(log_keep[:, 0:1])
        parts.extend(_split_bf16(log_keep) if split else [log_keep.astype(BF16)])
    r_all = _dot(jnp.concatenate(parts, axis=0), suffix)
    n_parts = 2 if split else 1
    weights, new_carries = [], []
    carry = carries if chained else None
    for i, (log_beta, mask) in enumerate(zip(log_betas, masks)):
        r = r_all[n_parts * i * tq:(n_parts * i + 1) * tq]
        if split:
            r = r + r_all[(2 * i + 1) * tq:(2 * i + 2) * tq]
        if not chained:
            carry = carries[i]
        carry_w = carry if tk == LANES else jnp.concatenate([carry] * (tk // LANES), axis=1)
        a = jnp.exp(log_beta + r[:, :tk] + carry_w)
        if mask is not None:
            a = jnp.where(mask, a, 0.0)
        weights.append(a.astype(BF16))
        if suffix.shape[1] > tk:
            tile_total = r[:, tk:]
        else:
            tile_total = jnp.broadcast_to(r[:, 0:1] + first_cols[i], (tq, LANES))
        carry = carry + tile_total
        new_carries.append(carry)
    return weights, (carry if chained else new_carries)


def _sb_prompt_kernel(bias_ref, q_ref, k_ref, v_ref, o_ref, acc_ref, carry_ref, *, tq, heads):
    hg = pl.program_id(1)
    i = pl.program_id(2)
    suffix = _suffix_matrix(tq, with_totals=False)
    row = lax.broadcasted_iota(jnp.int32, (tq, tq), 0)
    col = lax.broadcasted_iota(jnp.int32, (tq, tq), 1)
    diag_mask = col < row

    def tile(j, mask, first):
        start = pl.multiple_of(j * tq, tq)
        cols = [slice(hh * SB_HD, (hh + 1) * SB_HD) for hh in range(heads)]
        qks = [_dot_nt(q_ref[:, cs], k_ref[pl.ds(start, tq), cs]) for cs in cols]
        biases = [bias_ref[hg * heads + hh] for hh in range(heads)]
        carries = [jnp.zeros((tq, LANES), F32) if first else carry_ref[hh] for hh in range(heads)]
        weights, carries = _sb_weights(qks, biases, carries, suffix, [mask] * heads, chained=False, split=False)
        for hh, cs in enumerate(cols):
            av = _dot(weights[hh], v_ref[pl.ds(start, tq), cs])
            carry_ref[hh] = carries[hh]
            if first:
                acc_ref[hh] = av
            else:
                acc_ref[hh] += av

    tile(i, diag_mask, True)

    def body(t, _):
        tile(i - 1 - t, None, False)
        return 0

    lax.fori_loop(0, i, body, 0)
    for hh in range(heads):
        o_ref[:, hh * SB_HD:(hh + 1) * SB_HD] = acc_ref[hh].astype(o_ref.dtype)


def _sb_prompt(q, k, v, bias, *, batch, seq, tq, heads):
    m, n = q.shape
    nq = seq // tq
    w = heads * SB_HD
    assert seq % tq == 0 and n % w == 0 and tq % LANES == 0
    kern = functools.partial(_sb_prompt_kernel, tq=tq, heads=heads)
    return pl.pallas_call(
        kern,
        out_shape=jax.ShapeDtypeStruct((m, n), BF16),
        grid=(batch, n // w, nq),
        in_specs=[
            pl.BlockSpec(memory_space=pltpu.SMEM),
            pl.BlockSpec((tq, w), lambda b, h, i: (b * nq + i, h)),
            pl.BlockSpec((seq, w), lambda b, h, i: (b, h)),
            pl.BlockSpec((seq, w), lambda b, h, i: (b, h)),
        ],
        out_specs=pl.BlockSpec((tq, w), lambda b, h, i: (b * nq + i, h)),
        scratch_shapes=[pltpu.VMEM((heads, tq, SB_HD), F32), pltpu.VMEM((heads, tq, LANES), F32)],
        compiler_params=_params("parallel", "parallel", "arbitrary"),
        name="sb_prompt",
    )(bias, q, k, v)


def _sb_sample_kernel(pt_ref, q_ref, kn_ref, vn_ref, bias_ref, own_ref, fold_ref, spread_ref, *refs,
                      c, pages_per_step, page):
    k_refs = refs[:pages_per_step]
    v_refs = refs[pages_per_step:2 * pages_per_step]
    o_ref = refs[2 * pages_per_step]
    kpad_ref, vpad_ref, acc_ref, carry_ref = refs[2 * pages_per_step + 1:]
    jj = pl.program_id(1)
    groups = SB_HEADS // SB_HEAD_GROUP
    rows = SB_HEADS * c
    grows = SB_HEAD_GROUP * c
    prow = page * SB_HEAD_GROUP
    suffix = _suffix_matrix(page, with_totals=True)
    bias = bias_ref[...]
    own = own_ref[...]
    q = q_ref[0].astype(BF16)
    q_groups = [q[g * grows:(g + 1) * grows] for g in range(groups)]

    def page_groups(ref):
        return [ref[:, g].reshape(prow, SB_HD).astype(BF16) for g in range(groups)]

    def tiles(pks, pvs, carry, masks):
        n = len(pks)
        parts = []
        for pk in pks:
            for g in range(groups):
                parts.extend(_split_bf16(_dot_nt(q_groups[g], pk[g]) * own))
        folded = _dot(jnp.concatenate(parts, axis=0), fold_ref[...])
        qks = []
        for i in range(n):
            halves = []
            for g in range(groups):
                base = (i * groups + g) * 2 * grows
                halves.append(folded[base:base + grows] + folded[base + grows:base + 2 * grows])
            qks.append(jnp.concatenate(halves, axis=0))
        weights, carry = _sb_weights(qks, [bias] * n, carry, suffix, masks, chained=True, split=True)
        spread = _dot(jnp.concatenate(weights, axis=0), spread_ref[...])
        av = None
        for i, pv in enumerate(pvs):
            outs = []
            for g in range(groups):
                base = i * rows + g * grows
                a_wide = (spread[base:base + grows] * own).astype(BF16)
                outs.append(_dot(a_wide, pv[g]))
            term = jnp.concatenate(outs, axis=0)
            av = term if av is None else av + term
        return av, carry

    @pl.when(jj == 0)
    def _():
        kpad_ref[...] = jnp.zeros_like(kpad_ref)
        vpad_ref[...] = jnp.zeros_like(vpad_ref)
        kpad_ref[0:c] = kn_ref[0]
        vpad_ref[0:c] = vn_ref[0]
        r_idx = lax.broadcasted_iota(jnp.int32, (rows, page), 0)
        s_idx = lax.broadcasted_iota(jnp.int32, (rows, page), 1)
        av, carry = tiles([page_groups(kpad_ref)], [page_groups(vpad_ref)],
                          jnp.zeros((rows, LANES), F32), [s_idx < (r_idx % c)])
        acc_ref[...] = av
        carry_ref[...] = carry

    av, carry = tiles([page_groups(r.at[0]) for r in k_refs], [page_groups(r.at[0]) for r in v_refs],
                      carry_ref[...], [None] * pages_per_step)
    carry_ref[...] = carry
    acc_ref[...] += av

    @pl.when(jj == pl.num_programs(1) - 1)
    def _():
        o_ref[0] = acc_ref[...]


def _sb_sample(q, k_new, v_new, cache_k, cache_v, page_table, bias, *, pages_per_step):
    batch, rows, hd = q.shape
    c = rows // SB_HEADS
    n_pages = page_table.shape[1]
    page, groups, gh = cache_k.shape[1:4]
    prow = page * gh
    grows = gh * c
    steps = n_pages // pages_per_step
    assert n_pages % pages_per_step == 0 and page % LANES == 0 and gh == SB_HEAD_GROUP
    bias_rows = jnp.broadcast_to(jnp.repeat(bias.astype(F32), c)[:, None], (rows, page))
    own = (jnp.arange(prow)[None, :] % gh == jnp.arange(grows)[:, None] // c).astype(F32)
    fold = (jnp.arange(prow)[:, None] // gh == jnp.arange(page)[None, :]).astype(BF16)
    spread = fold.T

    def page_spec(p):
        return pl.BlockSpec(
            (1, page, groups, gh, hd),
            lambda b, jj, pt: (pt[b * n_pages + n_pages - 1 - (jj * pages_per_step + p)], 0, 0, 0, 0))

    const = lambda shape: pl.BlockSpec(shape, lambda b, jj, pt: (0,) * len(shape))
    new_spec = pl.BlockSpec((1, c, groups, gh, hd), lambda b, jj, pt: (b, 0, 0, 0, 0))
    kern = functools.partial(_sb_sample_kernel, c=c, pages_per_step=pages_per_step, page=page)
    grid_spec = pltpu.PrefetchScalarGridSpec(
        num_scalar_prefetch=1,
        grid=(batch, steps),
        in_specs=[pl.BlockSpec((1, rows, hd), lambda b, jj, pt: (b, 0, 0)), new_spec, new_spec,
                  const((rows, page)), const((grows, prow)), const((prow, page)), const((page, prow))]
        + [page_spec(p) for p in range(pages_per_step)] * 2,
        out_specs=pl.BlockSpec((1, rows, hd), lambda b, jj, pt: (b, 0, 0)),
        scratch_shapes=[pltpu.VMEM((page, groups, gh, hd), F32), pltpu.VMEM((page, groups, gh, hd), F32),
                        pltpu.VMEM((rows, hd), F32), pltpu.VMEM((rows, LANES), F32)],
    )
    return pl.pallas_call(
        kern,
        out_shape=jax.ShapeDtypeStruct((batch, rows, hd), F32),
        grid_spec=grid_spec,
        compiler_params=_params("parallel", "arbitrary"),
        name="sb_sample",
    )(page_table.reshape(-1), q, k_new, v_new, bias_rows, own, fold, spread,
      *([cache_k] * pages_per_step), *([cache_v] * pages_per_step))


def _rope_tables(pos, half):
    inv = ROPE_BASE ** (-jnp.arange(half, dtype=F32) / half)
    ang = pos[:, None] * inv[None, :]
    return jnp.cos(ang), jnp.sin(ang)


def kernel(x_prompt, x_sample, state_ret, cache_k, cache_v, page_table, ffn1_norm, ffn1_w_in, ffn1_w_out, mix_norm, ffn2_norm, ffn2_w_in, ffn2_w_out, ret_w_in, ret_gn, ret_w_out, kv_norm, w_kv, k_norm, sb_w_q, sb_q_norm, sb_bias, sb_w_out):
    batch, seq, d = x_prompt.shape
    dec_b, dec_s, _ = x_sample.shape
    n_pool, page = cache_k.shape[:2]
    past_len = page_table.shape[1] * page
    depth = ffn1_norm.shape[0]
    n_a = ret_w_in.shape[0]
    dk = d // RET_HEADS
    mp, ms = batch * seq, dec_b * dec_s

    bf = lambda w: w.astype(BF16)
    w_kv = bf(w_kv)
    ffn_w = ((bf(ffn1_w_in), bf(ffn1_w_out * FFN_RES)), (bf(ffn2_w_in), bf(ffn2_w_out * FFN_RES)))

    tm_p, tm_s = 512, ms
    tm_big = min(1024, mp)
    cos_p, sin_p = _rope_tables(jnp.arange(seq, dtype=F32), dk // 2)
    cos_s, sin_s = _rope_tables(past_len + jnp.arange(dec_s, dtype=F32), dk // 2)
    cos_s, sin_s = jnp.tile(cos_s, (dec_b, 1)), jnp.tile(sin_s, (dec_b, 1))

    xp = x_prompt.reshape(mp, d)
    xs = x_sample.reshape(ms, d)
    grouped = (SB_HEADS // SB_HEAD_GROUP, SB_HEAD_GROUP, SB_HD)
    cache_k2 = cache_k.reshape(n_pool, page, *grouped)
    cache_v2 = cache_v.reshape(n_pool, page, *grouped)

    def ffn_pair(xp, xs, norm, which, layer):
        w_in, w_out = ffn_w[which]
        return (_ffn(xp, norm, w_in, w_out, layer, tm=tm_big, tf=512),
                _ffn(xs, norm, w_in, w_out, layer, tm=tm_s, tf=512))

    ret_p, ret_s = [], []
    k_p = v_p = k_s = v_s = kb_p = vb_p = None
    for layer in range(depth):
        if layer == n_a:
            k_p, v_p, kb_p, vb_p = _shared_kv(xp, kv_norm, w_kv, k_norm, tm=tm_big, tn=512)
            k_s, v_s, _, _ = _shared_kv(xs, kv_norm, w_kv, k_norm, tm=tm_s, tn=512)

        xp, xs = ffn_pair(xp, xs, ffn1_norm[layer], 0, layer)

        if layer < n_a:
            a = layer
            w_in, w_out = bf(ret_w_in[a]), bf(ret_w_out[a])
            proj_p = _ret_in(xp, mix_norm[layer], w_in, cos_p, sin_p, tm=tm_big, tn=1024, out_dtype=BF16)
            proj_s = _ret_in(xs, mix_norm[layer], w_in, cos_s, sin_s, tm=tm_s, tn=1024, out_dtype=F32)
            og_p, st_p = _ret_prompt(proj_p, ret_gn[a], batch=batch, seq=seq)
            og_s, st_s = _ret_sample(proj_s.reshape(dec_b, dec_s, -1), state_ret[a], ret_gn[a])
            ret_p.append(st_p)
            ret_s.append(st_s)
            xp = _matmul_res(og_p, w_out, xp, tm=tm_big, tn=512)
            xs = _matmul_res(og_s.reshape(ms, -1), w_out, xs, tm=tm_s, tn=512)
        else:
            bl = layer - n_a
            w_q, w_out = bf(sb_w_q[bl]), bf(sb_w_out[bl])
            q_p = _sb_query(xp, mix_norm[layer], w_q, sb_q_norm[bl], tm=tm_big, tn=1024, out_dtype=BF16)
            q_s = _sb_query(xs, mix_norm[layer], w_q, sb_q_norm[bl], tm=tm_s, tn=1024, out_dtype=F32)
            at_p = _sb_prompt(q_p, kb_p, vb_p, sb_bias[bl], batch=batch, seq=seq, tq=256, heads=4)
            q_hq = q_s.reshape(dec_b, dec_s, SB_HEADS, SB_HD).transpose(0, 2, 1, 3)
            at_hq = _sb_sample(q_hq.reshape(dec_b, SB_HEADS * dec_s, SB_HD),
                               k_s.reshape(dec_b, dec_s, *grouped), v_s.reshape(dec_b, dec_s, *grouped),
                               cache_k2, cache_v2, page_table, sb_bias[bl], pages_per_step=8)
            at_s = at_hq.reshape(dec_b, SB_HEADS, dec_s, SB_HD).transpose(0, 2, 1, 3).reshape(ms, d)
            xp = _matmul_res(at_p, w_out, xp, tm=tm_big, tn=1024)
            xs = _matmul_res(at_s, w_out, xs, tm=tm_s, tn=1024)

        xp, xs = ffn_pair(xp, xs, ffn2_norm[layer], 1, layer)

    hd_shape = (SB_HEADS, SB_HD)
    return (xp.reshape(batch, seq, d), xs.reshape(dec_b, dec_s, d),
            jnp.stack(ret_p, axis=0), jnp.stack(ret_s, axis=0),
            k_p.reshape(batch, seq, *hd_shape), v_p.reshape(batch, seq, *hd_shape),
            k_s.reshape(dec_b, dec_s, *hd_shape), v_s.reshape(dec_b, dec_s, *hd_shape))
```

```python
import functools
import math

import jax
import jax.numpy as jnp
from jax import lax
from jax.experimental import pallas as pl
from jax.experimental.pallas import tpu as pltpu

F32 = jnp.float32
BF16 = jnp.bfloat16

EPS = 1e-6
LOG2_E = 1.0 / math.log(2.0)
FFN_RES = 0.5
ROPE_BASE = 10000.0
RET_HEADS = 8
RET_CHUNK = 128
SB_HEADS = 16
SB_HD = 128

LANES = 128
SUBLANES = 8
VMEM_LIMIT_BYTES = 56 * 1024 * 1024

SB_HEAD_GROUP = SUBLANES


def _params(*semantics):
    return pltpu.CompilerParams(dimension_semantics=semantics, vmem_limit_bytes=VMEM_LIMIT_BYTES)


def _rmsnorm_rows(x, gain):
    return x * lax.rsqrt(jnp.mean(x * x, axis=-1, keepdims=True) + EPS) * gain


def _head_rmsnorm(y, gain, hd):
    outs = []
    for c in range(y.shape[1] // hd):
        blk = y[:, c * hd:(c + 1) * hd]
        outs.append(blk * lax.rsqrt(jnp.mean(blk * blk, axis=-1, keepdims=True) + EPS)
                    * gain[:, c * hd:(c + 1) * hd])
    return outs


def _dot(a, b):
    return jnp.dot(a, b, preferred_element_type=F32)


def _dot_nt(a, b):
    return lax.dot_general(a, b, (((1,), (1,)), ((), ())), preferred_element_type=F32)


def _dot_tn(a, b):
    return lax.dot_general(a, b, (((0,), (0,)), ((), ())), preferred_element_type=F32)


def _ffn_kernel(x_ref, g_ref, wg_ref, wu_ref, wo_ref, o_ref, h_ref):
    j = pl.program_id(1)

    @pl.when(j == 0)
    def _():
        x = x_ref[...]
        h_ref[...] = _rmsnorm_rows(x, g_ref[...]).astype(BF16)
        o_ref[...] = x

    h = h_ref[...]
    gate = _dot(h, wg_ref[...])
    up = _dot(h, wu_ref[...])
    act = (gate * jax.nn.sigmoid(gate) * up).astype(BF16)
    o_ref[...] += _dot(act, wo_ref[...])


def _ffn(x, gain, w_in, w_out_scaled, layer, *, tm, tf):
    m, d = x.shape
    dff = w_out_scaled.shape[1]
    nj = dff // tf
    assert m % tm == 0 and dff % tf == 0
    return pl.pallas_call(
        _ffn_kernel,
        out_shape=jax.ShapeDtypeStruct((m, d), F32),
        grid=(m // tm, nj),
        in_specs=[
            pl.BlockSpec((tm, d), lambda i, j: (i, 0)),
            pl.BlockSpec((1, d), lambda i, j: (0, 0)),
            pl.BlockSpec((None, d, tf), lambda i, j: (layer, 0, j)),
            pl.BlockSpec((None, d, tf), lambda i, j: (layer, 0, j + nj)),
            pl.BlockSpec((None, tf, d), lambda i, j: (layer, j, 0)),
        ],
        out_specs=pl.BlockSpec((tm, d), lambda i, j: (i, 0)),
        scratch_shapes=[pltpu.VMEM((tm, d), BF16)],
        compiler_params=_params("parallel", "arbitrary"),
        name="ffn",
    )(x, gain.reshape(1, d), w_in, w_in, w_out_scaled)


def _ret_in_kernel(x_ref, g_ref, w_ref, cos_ref, sin_ref, o_ref, h_ref, *, nq, nk, dk, k_scale):
    j = pl.program_id(1)

    @pl.when(j == 0)
    def _():
        h_ref[...] = _rmsnorm_rows(x_ref[...], g_ref[...]).astype(BF16)

    y = _dot(h_ref[...], w_ref[...])
    half = dk // 2

    @pl.when(j < nq + nk)
    def _():
        sc = jnp.where(j < nq, 1.0, k_scale).astype(F32)
        cos = cos_ref[...]
        sin = sin_ref[...]
        for hh in range(y.shape[1] // dk):
            x1 = y[:, hh * dk:hh * dk + half]
            x2 = y[:, hh * dk + half:(hh + 1) * dk]
            o_ref[:, hh * dk:hh * dk + half] = ((x1 * cos - x2 * sin) * sc).astype(o_ref.dtype)
            o_ref[:, hh * dk + half:(hh + 1) * dk] = ((x1 * sin + x2 * cos) * sc).astype(o_ref.dtype)

    @pl.when(j >= nq + nk)
    def _():
        o_ref[...] = y.astype(o_ref.dtype)


def _ret_in(x, gain, w, cos, sin, *, tm, tn, out_dtype):
    m, d = x.shape
    n = w.shape[1]
    dk = d // RET_HEADS
    nper = cos.shape[0] // tm
    assert m % tm == 0 and n % tn == 0 and tn % dk == 0 and cos.shape[0] % tm == 0
    kern = functools.partial(_ret_in_kernel, nq=d // tn, nk=d // tn, dk=dk, k_scale=dk ** -0.5)
    return pl.pallas_call(
        kern,
        out_shape=jax.ShapeDtypeStruct((m, n), out_dtype),
        grid=(m // tm, n // tn),
        in_specs=[
            pl.BlockSpec((tm, d), lambda i, j: (i, 0)),
            pl.BlockSpec((1, d), lambda i, j: (0, 0)),
            pl.BlockSpec((d, tn), lambda i, j: (0, j)),
            pl.BlockSpec((tm, dk // 2), lambda i, j: (i % nper, 0)),
            pl.BlockSpec((tm, dk // 2), lambda i, j: (i % nper, 0)),
        ],
        out_specs=pl.BlockSpec((tm, tn), lambda i, j: (i, j)),
        scratch_shapes=[pltpu.VMEM((tm, d), BF16)],
        compiler_params=_params("parallel", "arbitrary"),
        name="ret_in",
    )(x, gain.reshape(1, d), w, cos, sin)


def _retention_heads(q_ref, k_ref, v_ref, g_ref, gn_ref, og_ref, get_state, put_state, *, c, dk, dv):
    row = lax.broadcasted_iota(jnp.int32, (c, c), 0)
    col = lax.broadcasted_iota(jnp.int32, (c, c), 1)
    diff = (row - col).astype(F32)
    idx = lax.broadcasted_iota(jnp.int32, (c, 1), 0).astype(F32)
    for h in range(RET_HEADS):
        lg = math.log(1.0 - 2.0 ** (-5.0 - h))
        q = q_ref[:, h * dk:(h + 1) * dk]
        k = k_ref[:, h * dk:(h + 1) * dk]
        v = v_ref[:, h * dv:(h + 1) * dv].astype(BF16)
        state = get_state(h)
        decay = jnp.where(diff >= 0, jnp.exp(lg * jnp.maximum(diff, 0.0)), 0.0)
        scores = _dot_nt(q.astype(BF16), k.astype(BF16)) * decay
        inner = _dot(scores.astype(BF16), v)
        q_decay = jnp.exp((idx + 1.0) * lg)
        cross = _dot((q.astype(F32) * q_decay).astype(BF16), state.astype(BF16))
        k_decay = jnp.exp((c - 1.0 - idx) * lg)
        kd = (k.astype(F32) * k_decay).astype(BF16)
        put_state(h, math.exp(c * lg) * state + _dot_tn(kd, v))
        o = inner + cross
        on = o * lax.rsqrt(jnp.mean(o * o, axis=-1, keepdims=True) + EPS) * gn_ref[:, h * dv:(h + 1) * dv]
        g = g_ref[:, h * dv:(h + 1) * dv].astype(F32)
        og_ref[:, h * dv:(h + 1) * dv] = (on * (g * jax.nn.sigmoid(g))).astype(og_ref.dtype)


def _ret_prompt_kernel(q_ref, k_ref, v_ref, g_ref, gn_ref, og_ref, st_ref, *, c, dk, dv):
    @pl.when(pl.program_id(1) == 0)
    def _():
        st_ref[...] = jnp.zeros_like(st_ref)

    def get_state(h):
        return st_ref[0, h]

    def put_state(h, s):
        st_ref[0, h] = s

    _retention_heads(q_ref, k_ref, v_ref, g_ref, gn_ref, og_ref, get_state, put_state, c=c, dk=dk, dv=dv)


def _ret_prompt(proj, gn, *, batch, seq):
    m, n = proj.shape
    d = n // 6
    dk, dv, c = d // RET_HEADS, 2 * d // RET_HEADS, RET_CHUNK
    nc = seq // c
    kern = functools.partial(_ret_prompt_kernel, c=c, dk=dk, dv=dv)
    og, st = pl.pallas_call(
        kern,
        out_shape=(jax.ShapeDtypeStruct((m, 2 * d), BF16),
                   jax.ShapeDtypeStruct((batch, RET_HEADS, dk, dv), F32)),
        grid=(batch, nc),
        in_specs=[
            pl.BlockSpec((c, d), lambda b, t: (b * nc + t, 0)),
            pl.BlockSpec((c, d), lambda b, t: (b * nc + t, 1)),
            pl.BlockSpec((c, 2 * d), lambda b, t: (b * nc + t, 1)),
            pl.BlockSpec((c, 2 * d), lambda b, t: (b * nc + t, 2)),
            pl.BlockSpec((1, 2 * d), lambda b, t: (0, 0)),
        ],
        out_specs=(pl.BlockSpec((c, 2 * d), lambda b, t: (b * nc + t, 0)),
                   pl.BlockSpec((1, RET_HEADS, dk, dv), lambda b, t: (b, 0, 0, 0))),
        compiler_params=_params("parallel", "arbitrary"),
        name="ret_prompt",
    )(proj, proj, proj, proj, gn.reshape(1, 2 * d))
    return og, st


def _ret_sample_kernel(q_ref, k_ref, v_ref, g_ref, gn_ref, st_in_ref, og_ref, st_ref, *, c, dk, dv):
    def get_state(h):
        return st_in_ref[0, h]

    def put_state(h, s):
        st_ref[0, h] = s

    _retention_heads(q_ref.at[0], k_ref.at[0], v_ref.at[0], g_ref.at[0], gn_ref, og_ref.at[0],
                     get_state, put_state, c=c, dk=dk, dv=dv)


def _ret_sample(proj, state, gn):
    batch, c, n = proj.shape
    d = n // 6
    dk, dv = d // RET_HEADS, 2 * d // RET_HEADS
    kern = functools.partial(_ret_sample_kernel, c=c, dk=dk, dv=dv)
    og, st = pl.pallas_call(
        kern,
        out_shape=(jax.ShapeDtypeStruct((batch, c, 2 * d), F32),
                   jax.ShapeDtypeStruct((batch, RET_HEADS, dk, dv), F32)),
        grid=(batch,),
        in_specs=[
            pl.BlockSpec((1, c, d), lambda b: (b, 0, 0)),
            pl.BlockSpec((1, c, d), lambda b: (b, 0, 1)),
            pl.BlockSpec((1, c, 2 * d), lambda b: (b, 0, 1)),
            pl.BlockSpec((1, c, 2 * d), lambda b: (b, 0, 2)),
            pl.BlockSpec((1, 2 * d), lambda b: (0, 0)),
            pl.BlockSpec((1, RET_HEADS, dk, dv), lambda b: (b, 0, 0, 0)),
        ],
        out_specs=(pl.BlockSpec((1, c, 2 * d), lambda b: (b, 0, 0)),
                   pl.BlockSpec((1, RET_HEADS, dk, dv), lambda b: (b, 0, 0, 0))),
        compiler_params=_params("parallel"),
        name="ret_sample",
    )(proj, proj, proj, proj, gn.reshape(1, 2 * d), state)
    return og, st


def _matmul_res_kernel(a_ref, w_ref, r_ref, o_ref):
    o_ref[...] = r_ref[...] + _dot(a_ref[...].astype(BF16), w_ref[...])


def _matmul_res(a, w, res, *, tm, tn):
    m, k = a.shape
    n = w.shape[1]
    assert m % tm == 0 and n % tn == 0
    return pl.pallas_call(
        _matmul_res_kernel,
        out_shape=jax.ShapeDtypeStruct((m, n), F32),
        grid=(m // tm, n // tn),
        in_specs=[
            pl.BlockSpec((tm, k), lambda i, j: (i, 0)),
            pl.BlockSpec((k, tn), lambda i, j: (0, j)),
            pl.BlockSpec((tm, tn), lambda i, j: (i, j)),
        ],
        out_specs=pl.BlockSpec((tm, tn), lambda i, j: (i, j)),
        compiler_params=_params("parallel", "arbitrary"),
        name="matmul_res",
    )(a, w, res)


def _kv_kernel(x_ref, g_ref, wk_ref, wv_ref, kn_ref, k_ref, v_ref, kb_ref, vb_ref, h_ref):
    @pl.when(pl.program_id(1) == 0)
    def _():
        h_ref[...] = _rmsnorm_rows(x_ref[...], g_ref[...]).astype(BF16)

    h = h_ref[...]
    yk = _dot(h, wk_ref[...])
    for c, blk in enumerate(_head_rmsnorm(yk, kn_ref[...], SB_HD)):
        k_ref[:, c * SB_HD:(c + 1) * SB_HD] = blk
        kb_ref[:, c * SB_HD:(c + 1) * SB_HD] = blk.astype(BF16)
    yv = _dot(h, wv_ref[...])
    v_ref[...] = yv
    vb_ref[...] = yv.astype(BF16)


def _shared_kv(x, gain, w_kv, k_norm, *, tm, tn):
    m, d = x.shape
    n = w_kv.shape[1] // 2
    nj = n // tn
    assert m % tm == 0 and n % tn == 0 and tn % SB_HD == 0
    blk = pl.BlockSpec((tm, tn), lambda i, j: (i, j))
    return pl.pallas_call(
        _kv_kernel,
        out_shape=(jax.ShapeDtypeStruct((m, n), F32), jax.ShapeDtypeStruct((m, n), F32),
                   jax.ShapeDtypeStruct((m, n), BF16), jax.ShapeDtypeStruct((m, n), BF16)),
        grid=(m // tm, nj),
        in_specs=[
            pl.BlockSpec((tm, d), lambda i, j: (i, 0)),
            pl.BlockSpec((1, d), lambda i, j: (0, 0)),
            pl.BlockSpec((d, tn), lambda i, j: (0, j)),
            pl.BlockSpec((d, tn), lambda i, j: (0, j + nj)),
            pl.BlockSpec((1, tn), lambda i, j: (0, j)),
        ],
        out_specs=(blk, blk, blk, blk),
        scratch_shapes=[pltpu.VMEM((tm, d), BF16)],
        compiler_params=_params("parallel", "arbitrary"),
        name="shared_kv",
    )(x, gain.reshape(1, d), w_kv, w_kv, k_norm.reshape(1, n))


def _sbq_kernel(x_ref, g_ref, w_ref, qn_ref, o_ref, h_ref):
    @pl.when(pl.program_id(1) == 0)
    def _():
        h_ref[...] = _rmsnorm_rows(x_ref[...], g_ref[...]).astype(BF16)

    y = _dot(h_ref[...], w_ref[...])
    for c, blk in enumerate(_head_rmsnorm(y, qn_ref[...], SB_HD)):
        o_ref[:, c * SB_HD:(c + 1) * SB_HD] = blk.astype(o_ref.dtype)


def _sb_query(x, gain, w_q, q_norm, *, tm, tn, out_dtype):
    m, d = x.shape
    n = w_q.shape[1]
    assert m % tm == 0 and n % tn == 0 and tn % SB_HD == 0
    return pl.pallas_call(
        _sbq_kernel,
        out_shape=jax.ShapeDtypeStruct((m, n), out_dtype),
        grid=(m // tm, n // tn),
        in_specs=[
            pl.BlockSpec((tm, d), lambda i, j: (i, 0)),
            pl.BlockSpec((1, d), lambda i, j: (0, 0)),
            pl.BlockSpec((d, tn), lambda i, j: (0, j)),
            pl.BlockSpec((1, tn), lambda i, j: (0, j)),
        ],
        out_specs=pl.BlockSpec((tm, tn), lambda i, j: (i, j)),
        scratch_shapes=[pltpu.VMEM((tm, d), BF16)],
        compiler_params=_params("parallel", "arbitrary"),
        name="sb_query",
    )(x, gain.reshape(1, d), w_q, q_norm.reshape(1, n))


def _suffix_matrix(tk, with_totals):
    width = tk + LANES if with_totals else tk
    j = lax.broadcasted_iota(jnp.int32, (tk, width), 0)
    s = lax.broadcasted_iota(jnp.int32, (tk, width), 1)
    return jnp.where((j > s) | (s >= tk), 1.0, 0.0).astype(BF16)


def _split_bf16(x):
    hi = x.astype(BF16)
    return hi, (x - hi.astype(F32)).astype(BF16)


def _sb_weights(qks, biases, carries, suffix, masks, chained, split):
    tq, tk = qks[0].shape
    log_betas, first_cols, parts = [], [], []
    for qk, bias, mask in zip(qks, biases, masks):
        z = qk * (SB_HD ** -0.5) + bias
        t = jnp.log(1.0 + jnp.exp2(jnp.abs(z) * -LOG2_E))
        log_beta = jnp.minimum(z, 0.0) - t
        log_betas.append(log_beta)
        log_keep = log_beta - z
        if mask is not None:
            log_keep = jnp.where(mask, log_keep, 0.0)
        first_cols.append(log_keep[:, 0:1])
        parts.extend(_split_bf16(log_keep) if split else [log_keep.astype(BF16)])
    r_all = _dot(jnp.concatenate(parts, axis=0), suffix)
    n_parts = 2 if split else 1
    weights, new_carries = [], []
    carry = carries if chained else None
    for i, (log_beta, mask) in enumerate(zip(log_betas, masks)):
        r = r_all[n_parts * i * tq:(n_parts * i + 1) * tq]
        if split:
            r = r + r_all[(2 * i + 1) * tq:(2 * i + 2) * tq]
        if not chained:
            carry = carries[i]
        carry_w = carry if tk == LANES else jnp.concatenate([carry] * (tk // LANES), axis=1)
        a = jnp.exp(log_beta + r[:, :tk] + carry_w)
        if mask is not None:
            a = jnp.where(mask, a, 0.0)
        weights.append(a.astype(BF16))
        if suffix.shape[1] > tk:
            tile_total = r[:, tk:]
        else:
            tile_total = jnp.broadcast_to(r[:, 0:1] + first_cols[i], (tq, LANES))
        carry = carry + tile_total
        new_carries.append(carry)
    return weights, (carry if chained else new_carries)


def _sb_prompt_kernel(bias_ref, q_ref, k_ref, v_ref, o_ref, acc_ref, carry_ref, *, tq, heads):
    hg = pl.program_id(1)
    i = pl.program_id(2)
    suffix = _suffix_matrix(tq, with_totals=False)
    row = lax.broadcasted_iota(jnp.int32, (tq, tq), 0)
    col = lax.broadcasted_iota(jnp.int32, (tq, tq), 1)
    diag_mask = col < row

    def tile(j, mask, first):
        start = pl.multiple_of(j * tq, tq)
        cols = [slice(hh * SB_HD, (hh + 1) * SB_HD) for hh in range(heads)]
        qks = [_dot_nt(q_ref[:, cs], k_ref[pl.ds(start, tq), cs]) for cs in cols]
        biases = [bias_ref[hg * heads + hh] for hh in range(heads)]
        carries = [jnp.zeros((tq, LANES), F32) if first else carry_ref[hh] for hh in range(heads)]
        weights, carries = _sb_weights(qks, biases, carries, suffix, [mask] * heads, chained=False, split=False)
        for hh, cs in enumerate(cols):
            av = _dot(weights[hh], v_ref[pl.ds(start, tq), cs])
            carry_ref[hh] = carries[hh]
            if first:
                acc_ref[hh] = av
            else:
                acc_ref[hh] += av

    tile(i, diag_mask, True)

    def body(t, _):
        tile(i - 1 - t, None, False)
        return 0

    lax.fori_loop(0, i, body, 0)
    for hh in range(heads):
        o_ref[:, hh * SB_HD:(hh + 1) * SB_HD] = acc_ref[hh].astype(o_ref.dtype)


def _sb_prompt(q, k, v, bias, *, batch, seq, tq, heads):
    m, n = q.shape
    nq = seq // tq
    w = heads * SB_HD
    assert seq % tq == 0 and n % w == 0 and tq % LANES == 0
    kern = functools.partial(_sb_prompt_kernel, tq=tq, heads=heads)
    return pl.pallas_call(
        kern,
        out_shape=jax.ShapeDtypeStruct((m, n), BF16),
        grid=(batch, n // w, nq),
        in_specs=[
            pl.BlockSpec(memory_space=pltpu.SMEM),
            pl.BlockSpec((tq, w), lambda b, h, i: (b * nq + i, h)),
            pl.BlockSpec((seq, w), lambda b, h, i: (b, h)),
            pl.BlockSpec((seq, w), lambda b, h, i: (b, h)),
        ],
        out_specs=pl.BlockSpec((tq, w), lambda b, h, i: (b * nq + i, h)),
        scratch_shapes=[pltpu.VMEM((heads, tq, SB_HD), F32), pltpu.VMEM((heads, tq, LANES), F32)],
        compiler_params=_params("parallel", "parallel", "arbitrary"),
        name="sb_prompt",
    )(bias, q, k, v)


def _sb_sample_kernel(pt_ref, q_ref, kn_ref, vn_ref, bias_ref, own_ref, fold_ref, spread_ref, *refs,
                      c, pages_per_step, page):
    k_refs = refs[:pages_per_step]
    v_refs = refs[pages_per_step:2 * pages_per_step]
    o_ref = refs[2 * pages_per_step]
    kpad_ref, vpad_ref, acc_ref, carry_ref = refs[2 * pages_per_step + 1:]
    jj = pl.program_id(1)
    groups = SB_HEADS // SB_HEAD_GROUP
    rows = SB_HEADS * c
    grows = SB_HEAD_GROUP * c
    prow = page * SB_HEAD_GROUP
    suffix = _suffix_matrix(page, with_totals=True)
    bias = bias_ref[...]
    own = own_ref[...]
    q = q_ref[0].astype(BF16)
    q_groups = [q[g * grows:(g + 1) * grows] for g in range(groups)]

    def page_groups(ref):
        return [ref[:, g].reshape(prow, SB_HD).astype(BF16) for g in range(groups)]

    def tiles(pks, pvs, carry, masks):
        n = len(pks)
        parts = []
        for pk in pks:
            for g in range(groups):
                parts.extend(_split_bf16(_dot_nt(q_groups[g], pk[g]) * own))
        folded = _dot(jnp.concatenate(parts, axis=0), fold_ref[...])
        qks = []
        for i in range(n):
            halves = []
            for g in range(groups):
                base = (i * groups + g) * 2 * grows
                halves.append(folded[base:base + grows] + folded[base + grows:base + 2 * grows])
            qks.append(jnp.concatenate(halves, axis=0))
        weights, carry = _sb_weights(qks, [bias] * n, carry, suffix, masks, chained=True, split=True)
        spread = _dot(jnp.concatenate(weights, axis=0), spread_ref[...])
        av = None
        for i, pv in enumerate(pvs):
            outs = []
            for g in range(groups):
                base = i * rows + g * grows
                a_wide = (spread[base:base + grows] * own).astype(BF16)
                outs.append(_dot(a_wide, pv[g]))
            term = jnp.concatenate(outs, axis=0)
            av = term if av is None else av + term
        return av, carry

    @pl.when(jj == 0)
    def _():
        kpad_ref[...] = jnp.zeros_like(kpad_ref)
        vpad_ref[...] = jnp.zeros_like(vpad_ref)
        kpad_ref[0:c] = kn_ref[0]
        vpad_ref[0:c] = vn_ref[0]
        r_idx = lax.broadcasted_iota(jnp.int32, (rows, page), 0)
        s_idx = lax.broadcasted_iota(jnp.int32, (rows, page), 1)
        av, carry = tiles([page_groups(kpad_ref)], [page_groups(vpad_ref)],
                          jnp.zeros((rows, LANES), F32), [s_idx < (r_idx % c)])
        acc_ref[...] = av
        carry_ref[...] = carry

    av, carry = tiles([page_groups(r.at[0]) for r in k_refs], [page_groups(r.at[0]) for r in v_refs],
                      carry_ref[...], [None] * pages_per_step)
    carry_ref[...] = carry
    acc_ref[...] += av

    @pl.when(jj == pl.num_programs(1) - 1)
    def _():
        o_ref[0] = acc_ref[...]


def _sb_sample(q, k_new, v_new, cache_k, cache_v, page_table, bias, *, pages_per_step):
    batch, rows, hd = q.shape
    c = rows // SB_HEADS
    n_pages = page_table.shape[1]
    page, groups, gh = cache_k.shape[1:4]
    prow = page * gh
    grows = gh * c
    steps = n_pages // pages_per_step
    assert n_pages % pages_per_step == 0 and page % LANES == 0 and gh == SB_HEAD_GROUP
    bias_rows = jnp.broadcast_to(jnp.repeat(bias.astype(F32), c)[:, None], (rows, page))
    own = (jnp.arange(prow)[None, :] % gh == jnp.arange(grows)[:, None] // c).astype(F32)
    fold = (jnp.arange(prow)[:, None] // gh == jnp.arange(page)[None, :]).astype(BF16)
    spread = fold.T

    def page_spec(p):
        return pl.BlockSpec(
            (1, page, groups, gh, hd),
            lambda b, jj, pt: (pt[b * n_pages + n_pages - 1 - (jj * pages_per_step + p)], 0, 0, 0, 0))

    const = lambda shape: pl.BlockSpec(shape, lambda b, jj, pt: (0,) * len(shape))
    new_spec = pl.BlockSpec((1, c, groups, gh, hd), lambda b, jj, pt: (b, 0, 0, 0, 0))
    kern = functools.partial(_sb_sample_kernel, c=c, pages_per_step=pages_per_step, page=page)
    grid_spec = pltpu.PrefetchScalarGridSpec(
        num_scalar_prefetch=1,
        grid=(batch, steps),
        in_specs=[pl.BlockSpec((1, rows, hd), lambda b, jj, pt: (b, 0, 0)), new_spec, new_spec,
                  const((rows, page)), const((grows, prow)), const((prow, page)), const((page, prow))]
        + [page_spec(p) for p in range(pages_per_step)] * 2,
        out_specs=pl.BlockSpec((1, rows, hd), lambda b, jj, pt: (b, 0, 0)),
        scratch_shapes=[pltpu.VMEM((page, groups, gh, hd), F32), pltpu.VMEM((page, groups, gh, hd), F32),
                        pltpu.VMEM((rows, hd), F32), pltpu.VMEM((rows, LANES), F32)],
    )
    return pl.pallas_call(
        kern,
        out_shape=jax.ShapeDtypeStruct((batch, rows, hd), F32),
        grid_spec=grid_spec,
        compiler_params=_params("parallel", "arbitrary"),
        name="sb_sample",
    )(page_table.reshape(-1), q, k_new, v_new, bias_rows, own, fold, spread,
      *([cache_k] * pages_per_step), *([cache_v] * pages_per_step))


def _rope_tables(pos, half):
    inv = ROPE_BASE ** (-jnp.arange(half, dtype=F32) / half)
    ang = pos[:, None] * inv[None, :]
    return jnp.cos(ang), jnp.sin(ang)


def kernel(x_prompt, x_sample, state_ret, cache_k, cache_v, page_table, ffn1_norm, ffn1_w_in, ffn1_w_out, mix_norm, ffn2_norm, ffn2_w_in, ffn2_w_out, ret_w_in, ret_gn, ret_w_out, kv_norm, w_kv, k_norm, sb_w_q, sb_q_norm, sb_bias, sb_w_out):
    batch, seq, d = x_prompt.shape
    dec_b, dec_s, _ = x_sample.shape
    n_pool, page = cache_k.shape[:2]
    past_len = page_table.shape[1] * page
    depth = ffn1_norm.shape[0]
    n_a = ret_w_in.shape[0]
    dk = d // RET_HEADS
    mp, ms = batch * seq, dec_b * dec_s

    bf = lambda w: w.astype(BF16)
    w_kv = bf(w_kv)
    ffn_w = ((bf(ffn1_w_in), bf(ffn1_w_out * FFN_RES)), (bf(ffn2_w_in), bf(ffn2_w_out * FFN_RES)))

    tm_p, tm_s = 512, ms
    tm_big = min(1024, mp)
    cos_p, sin_p = _rope_tables(jnp.arange(seq, dtype=F32), dk // 2)
    cos_s, sin_s = _rope_tables(past_len + jnp.arange(dec_s, dtype=F32), dk // 2)
    cos_s, sin_s = jnp.tile(cos_s, (dec_b, 1)), jnp.tile(sin_s, (dec_b, 1))

    xp = x_prompt.reshape(mp, d)
    xs = x_sample.reshape(ms, d)
    grouped = (SB_HEADS // SB_HEAD_GROUP, SB_HEAD_GROUP, SB_HD)
    cache_k2 = cache_k.reshape(n_pool, page, *grouped)
    cache_v2 = cache_v.reshape(n_pool, page, *grouped)

    def ffn_pair(xp, xs, norm, which, layer):
        w_in, w_out = ffn_w[which]
        return (_ffn(xp, norm, w_in, w_out, layer, tm=tm_big, tf=512),
                _ffn(xs, norm, w_in, w_out, layer, tm=tm_s, tf=512))

    ret_p, ret_s = [], []
    k_p = v_p = k_s = v_s = kb_p = vb_p = None
    for layer in range(depth):
        if layer == n_a:
            k_p, v_p, kb_p, vb_p = _shared_kv(xp, kv_norm, w_kv, k_norm, tm=tm_big, tn=512)
            k_s, v_s, _, _ = _shared_kv(xs, kv_norm, w_kv, k_norm, tm=tm_s, tn=512)

        xp, xs = ffn_pair(xp, xs, ffn1_norm[layer], 0, layer)

        if layer < n_a:
            a = layer
            w_in, w_out = bf(ret_w_in[a]), bf(ret_w_out[a])
            proj_p = _ret_in(xp, mix_norm[layer], w_in, cos_p, sin_p, tm=tm_big, tn=1024, out_dtype=BF16)
            proj_s = _ret_in(xs, mix_norm[layer], w_in, cos_s, sin_s, tm=tm_s, tn=1024, out_dtype=F32)
            og_p, st_p = _ret_prompt(proj_p, ret_gn[a], batch=batch, seq=seq)
            og_s, st_s = _ret_sample(proj_s.reshape(dec_b, dec_s, -1), state_ret[a], ret_gn[a])
            ret_p.append(st_p)
            ret_s.append(st_s)
            xp = _matmul_res(og_p, w_out, xp, tm=tm_big, tn=512)
            xs = _matmul_res(og_s.reshape(ms, -1), w_out, xs, tm=tm_s, tn=512)
        else:
            bl = layer - n_a
            w_q, w_out = bf(sb_w_q[bl]), bf(sb_w_out[bl])
            q_p = _sb_query(xp, mix_norm[layer], w_q, sb_q_norm[bl], tm=tm_big, tn=2048, out_dtype=BF16)
            q_s = _sb_query(xs, mix_norm[layer], w_q, sb_q_norm[bl], tm=tm_s, tn=1024, out_dtype=F32)
            at_p = _sb_prompt(q_p, kb_p, vb_p, sb_bias[bl], batch=batch, seq=seq, tq=256, heads=8)
            q_hq = q_s.reshape(dec_b, dec_s, SB_HEADS, SB_HD).transpose(0, 2, 1, 3)
            at_hq = _sb_sample(q_hq.reshape(dec_b, SB_HEADS * dec_s, SB_HD),
                               k_s.reshape(dec_b, dec_s, *grouped), v_s.reshape(dec_b, dec_s, *grouped),
                               cache_k2, cache_v2, page_table, sb_bias[bl], pages_per_step=8)
            at_s = at_hq.reshape(dec_b, SB_HEADS, dec_s, SB_HD).transpose(0, 2, 1, 3).reshape(ms, d)
            xp = _matmul_res(at_p, w_out, xp, tm=tm_big, tn=1024)
            xs = _matmul_res(at_s, w_out, xs, tm=tm_s, tn=1024)

        xp, xs = ffn_pair(xp, xs, ffn2_norm[layer], 1, layer)

    hd_shape = (SB_HEADS, SB_HD)
    return (xp.reshape(batch, seq, d), xs.reshape(dec_b, dec_s, d),
            jnp.stack(ret_p, axis=0), jnp.stack(ret_s, axis=0),
            k_p.reshape(batch, seq, *hd_shape), v_p.reshape(batch, seq, *hd_shape),
            k_s.reshape(dec_b, dec_s, *hd_shape), v_s.reshape(dec_b, dec_s, *hd_shape))
```

```python
import functools
import math

import jax
import jax.numpy as jnp
from jax import lax
from jax.experimental import pallas as pl
from jax.experimental.pallas import tpu as pltpu

F32 = jnp.float32
BF16 = jnp.bfloat16

EPS = 1e-6
LOG2_E = 1.0 / math.log(2.0)
FFN_RES = 0.5
ROPE_BASE = 10000.0
RET_HEADS = 8
RET_CHUNK = 128
SB_HEADS = 16
SB_HD = 128

LANES = 128
SUBLANES = 8
VMEM_LIMIT_BYTES = 56 * 1024 * 1024

SB_HEAD_GROUP = SUBLANES


def _params(*semantics):
    return pltpu.CompilerParams(dimension_semantics=semantics, vmem_limit_bytes=VMEM_LIMIT_BYTES)


def _rmsnorm_rows(x, gain):
    return x * lax.rsqrt(jnp.mean(x * x, axis=-1, keepdims=True) + EPS) * gain


def _head_rmsnorm(y, gain, hd):
    outs = []
    for c in range(y.shape[1] // hd):
        blk = y[:, c * hd:(c + 1) * hd]
        outs.append(blk * lax.rsqrt(jnp.mean(blk * blk, axis=-1, keepdims=True) + EPS)
                    * gain[:, c * hd:(c + 1) * hd])
    return outs


def _dot(a, b):
    return jnp.dot(a, b, preferred_element_type=F32)


def _dot_nt(a, b):
    return lax.dot_general(a, b, (((1,), (1,)), ((), ())), preferred_element_type=F32)


def _dot_tn(a, b):
    return lax.dot_general(a, b, (((0,), (0,)), ((), ())), preferred_element_type=F32)


def _ffn_kernel(x_ref, g_ref, wg_ref, wu_ref, wo_ref, o_ref, h_ref):
    j = pl.program_id(1)

    @pl.when(j == 0)
    def _():
        x = x_ref[...]
        h_ref[...] = _rmsnorm_rows(x, g_ref[...]).astype(BF16)
        o_ref[...] = x

    h = h_ref[...]
    gate = _dot(h, wg_ref[...])
    up = _dot(h, wu_ref[...])
    act = (gate * jax.nn.sigmoid(gate) * up).astype(BF16)
    o_ref[...] += _dot(act, wo_ref[...])


def _ffn(x, gain, w_in, w_out_scaled, layer, *, tm, tf):
    m, d = x.shape
    dff = w_out_scaled.shape[1]
    nj = dff // tf
    assert m % tm == 0 and dff % tf == 0
    return pl.pallas_call(
        _ffn_kernel,
        out_shape=jax.ShapeDtypeStruct((m, d), F32),
        grid=(m // tm, nj),
        in_specs=[
            pl.BlockSpec((tm, d), lambda i, j: (i, 0)),
            pl.BlockSpec((1, d), lambda i, j: (0, 0)),
            pl.BlockSpec((None, d, tf), lambda i, j: (layer, 0, j)),
            pl.BlockSpec((None, d, tf), lambda i, j: (layer, 0, j + nj)),
            pl.BlockSpec((None, tf, d), lambda i, j: (layer, j, 0)),
        ],
        out_specs=pl.BlockSpec((tm, d), lambda i, j: (i, 0)),
        scratch_shapes=[pltpu.VMEM((tm, d), BF16)],
        compiler_params=_params("parallel", "arbitrary"),
        name="ffn",
    )(x, gain.reshape(1, d), w_in, w_in, w_out_scaled)


def _ret_in_kernel(x_ref, g_ref, w_ref, cos_ref, sin_ref, o_ref, h_ref, *, nq, nk, dk, k_scale):
    j = pl.program_id(1)

    @pl.when(j == 0)
    def _():
        h_ref[...] = _rmsnorm_rows(x_ref[...], g_ref[...]).astype(BF16)

    y = _dot(h_ref[...], w_ref[...])
    half = dk // 2

    @pl.when(j < nq + nk)
    def _():
        sc = jnp.where(j < nq, 1.0, k_scale).astype(F32)
        cos = cos_ref[...]
        sin = sin_ref[...]
        for hh in range(y.shape[1] // dk):
            x1 = y[:, hh * dk:hh * dk + half]
            x2 = y[:, hh * dk + half:(hh + 1) * dk]
            o_ref[:, hh * dk:hh * dk + half] = ((x1 * cos - x2 * sin) * sc).astype(o_ref.dtype)
            o_ref[:, hh * dk + half:(hh + 1) * dk] = ((x1 * sin + x2 * cos) * sc).astype(o_ref.dtype)

    @pl.when(j >= nq + nk)
    def _():
        o_ref[...] = y.astype(o_ref.dtype)


def _ret_in(x, gain, w, cos, sin, *, tm, tn, out_dtype):
    m, d = x.shape
    n = w.shape[1]
    dk = d // RET_HEADS
    nper = cos.shape[0] // tm
    assert m % tm == 0 and n % tn == 0 and tn % dk == 0 and cos.shape[0] % tm == 0
    kern = functools.partial(_ret_in_kernel, nq=d // tn, nk=d // tn, dk=dk, k_scale=dk ** -0.5)
    return pl.pallas_call(
        kern,
        out_shape=jax.ShapeDtypeStruct((m, n), out_dtype),
        grid=(m // tm, n // tn),
        in_specs=[
            pl.BlockSpec((tm, d), lambda i, j: (i, 0)),
            pl.BlockSpec((1, d), lambda i, j: (0, 0)),
            pl.BlockSpec((d, tn), lambda i, j: (0, j)),
            pl.BlockSpec((tm, dk // 2), lambda i, j: (i % nper, 0)),
            pl.BlockSpec((tm, dk // 2), lambda i, j: (i % nper, 0)),
        ],
        out_specs=pl.BlockSpec((tm, tn), lambda i, j: (i, j)),
        scratch_shapes=[pltpu.VMEM((tm, d), BF16)],
        compiler_params=_params("parallel", "arbitrary"),
        name="ret_in",
    )(x, gain.reshape(1, d), w, cos, sin)


def _retention_heads(q_ref, k_ref, v_ref, g_ref, gn_ref, og_ref, get_state, put_state, *, c, dk, dv):
    row = lax.broadcasted_iota(jnp.int32, (c, c), 0)
    col = lax.broadcasted_iota(jnp.int32, (c, c), 1)
    diff = (row - col).astype(F32)
    idx = lax.broadcasted_iota(jnp.int32, (c, 1), 0).astype(F32)
    for h in range(RET_HEADS):
        lg = math.log(1.0 - 2.0 ** (-5.0 - h))
        q = q_ref[:, h * dk:(h + 1) * dk]
        k = k_ref[:, h * dk:(h + 1) * dk]
        v = v_ref[:, h * dv:(h + 1) * dv].astype(BF16)
        state = get_state(h)
        decay = jnp.where(diff >= 0, jnp.exp(lg * jnp.maximum(diff, 0.0)), 0.0)
        scores = _dot_nt(q.astype(BF16), k.astype(BF16)) * decay
        inner = _dot(scores.astype(BF16), v)
        q_decay = jnp.exp((idx + 1.0) * lg)
        cross = _dot((q.astype(F32) * q_decay).astype(BF16), state.astype(BF16))
        k_decay = jnp.exp((c - 1.0 - idx) * lg)
        kd = (k.astype(F32) * k_decay).astype(BF16)
        put_state(h, math.exp(c * lg) * state + _dot_tn(kd, v))
        o = inner + cross
        on = o * lax.rsqrt(jnp.mean(o * o, axis=-1, keepdims=True) + EPS) * gn_ref[:, h * dv:(h + 1) * dv]
        g = g_ref[:, h * dv:(h + 1) * dv].astype(F32)
        og_ref[:, h * dv:(h + 1) * dv] = (on * (g * jax.nn.sigmoid(g))).astype(og_ref.dtype)


def _ret_prompt_kernel(q_ref, k_ref, v_ref, g_ref, gn_ref, og_ref, st_ref, *, c, dk, dv):
    @pl.when(pl.program_id(1) == 0)
    def _():
        st_ref[...] = jnp.zeros_like(st_ref)

    def get_state(h):
        return st_ref[0, h]

    def put_state(h, s):
        st_ref[0, h] = s

    _retention_heads(q_ref, k_ref, v_ref, g_ref, gn_ref, og_ref, get_state, put_state, c=c, dk=dk, dv=dv)


def _ret_prompt(proj, gn, *, batch, seq):
    m, n = proj.shape
    d = n // 6
    dk, dv, c = d // RET_HEADS, 2 * d // RET_HEADS, RET_CHUNK
    nc = seq // c
    kern = functools.partial(_ret_prompt_kernel, c=c, dk=dk, dv=dv)
    og, st = pl.pallas_call(
        kern,
        out_shape=(jax.ShapeDtypeStruct((m, 2 * d), BF16),
                   jax.ShapeDtypeStruct((batch, RET_HEADS, dk, dv), F32)),
        grid=(batch, nc),
        in_specs=[
            pl.BlockSpec((c, d), lambda b, t: (b * nc + t, 0)),
            pl.BlockSpec((c, d), lambda b, t: (b * nc + t, 1)),
            pl.BlockSpec((c, 2 * d), lambda b, t: (b * nc + t, 1)),
            pl.BlockSpec((c, 2 * d), lambda b, t: (b * nc + t, 2)),
            pl.BlockSpec((1, 2 * d), lambda b, t: (0, 0)),
        ],
        out_specs=(pl.BlockSpec((c, 2 * d), lambda b, t: (b * nc + t, 0)),
                   pl.BlockSpec((1, RET_HEADS, dk, dv), lambda b, t: (b, 0, 0, 0))),
        compiler_params=_params("parallel", "arbitrary"),
        name="ret_prompt",
    )(proj, proj, proj, proj, gn.reshape(1, 2 * d))
    return og, st


def _ret_sample_kernel(q_ref, k_ref, v_ref, g_ref, gn_ref, st_in_ref, og_ref, st_ref, *, c, dk, dv):
    def get_state(h):
        return st_in_ref[0, h]

    def put_state(h, s):
        st_ref[0, h] = s

    _retention_heads(q_ref.at[0], k_ref.at[0], v_ref.at[0], g_ref.at[0], gn_ref, og_ref.at[0],
                     get_state, put_state, c=c, dk=dk, dv=dv)


def _ret_sample(proj, state, gn):
    batch, c, n = proj.shape
    d = n // 6
    dk, dv = d // RET_HEADS, 2 * d // RET_HEADS
    kern = functools.partial(_ret_sample_kernel, c=c, dk=dk, dv=dv)
    og, st = pl.pallas_call(
        kern,
        out_shape=(jax.ShapeDtypeStruct((batch, c, 2 * d), F32),
                   jax.ShapeDtypeStruct((batch, RET_HEADS, dk, dv), F32)),
        grid=(batch,),
        in_specs=[
            pl.BlockSpec((1, c, d), lambda b: (b, 0, 0)),
            pl.BlockSpec((1, c, d), lambda b: (b, 0, 1)),
            pl.BlockSpec((1, c, 2 * d), lambda b: (b, 0, 1)),
            pl.BlockSpec((1, c, 2 * d), lambda b: (b, 0, 2)),
            pl.BlockSpec((1, 2 * d), lambda b: (0, 0)),
            pl.BlockSpec((1, RET_HEADS, dk, dv), lambda b: (b, 0, 0, 0)),
        ],
        out_specs=(pl.BlockSpec((1, c, 2 * d), lambda b: (b, 0, 0)),
                   pl.BlockSpec((1, RET_HEADS, dk, dv), lambda b: (b, 0, 0, 0))),
        compiler_params=_params("parallel"),
        name="ret_sample",
    )(proj, proj, proj, proj, gn.reshape(1, 2 * d), state)
    return og, st


def _matmul_res_kernel(a_ref, w_ref, r_ref, o_ref):
    o_ref[...] = r_ref[...] + _dot(a_ref[...].astype(BF16), w_ref[...])


def _matmul_res(a, w, res, *, tm, tn):
    m, k = a.shape
    n = w.shape[1]
    assert m % tm == 0 and n % tn == 0
    return pl.pallas_call(
        _matmul_res_kernel,
        out_shape=jax.ShapeDtypeStruct((m, n), F32),
        grid=(m // tm, n // tn),
        in_specs=[
            pl.BlockSpec((tm, k), lambda i, j: (i, 0)),
            pl.BlockSpec((k, tn), lambda i, j: (0, j)),
            pl.BlockSpec((tm, tn), lambda i, j: (i, j)),
        ],
        out_specs=pl.BlockSpec((tm, tn), lambda i, j: (i, j)),
        compiler_params=_params("parallel", "arbitrary"),
        name="matmul_res",
    )(a, w, res)


def _kv_kernel(x_ref, g_ref, wk_ref, wv_ref, kn_ref, k_ref, v_ref, kb_ref, vb_ref, h_ref):
    @pl.when(pl.program_id(1) == 0)
    def _():
        h_ref[...] = _rmsnorm_rows(x_ref[...], g_ref[...]).astype(BF16)

    h = h_ref[...]
    yk = _dot(h, wk_ref[...])
    for c, blk in enumerate(_head_rmsnorm(yk, kn_ref[...], SB_HD)):
        k_ref[:, c * SB_HD:(c + 1) * SB_HD] = blk
        kb_ref[:, c * SB_HD:(c + 1) * SB_HD] = blk.astype(BF16)
    yv = _dot(h, wv_ref[...])
    v_ref[...] = yv
    vb_ref[...] = yv.astype(BF16)


def _shared_kv(x, gain, w_kv, k_norm, *, tm, tn):
    m, d = x.shape
    n = w_kv.shape[1] // 2
    nj = n // tn
    assert m % tm == 0 and n % tn == 0 and tn % SB_HD == 0
    blk = pl.BlockSpec((tm, tn), lambda i, j: (i, j))
    return pl.pallas_call(
        _kv_kernel,
        out_shape=(jax.ShapeDtypeStruct((m, n), F32), jax.ShapeDtypeStruct((m, n), F32),
                   jax.ShapeDtypeStruct((m, n), BF16), jax.ShapeDtypeStruct((m, n), BF16)),
        grid=(m // tm, nj),
        in_specs=[
            pl.BlockSpec((tm, d), lambda i, j: (i, 0)),
            pl.BlockSpec((1, d), lambda i, j: (0, 0)),
            pl.BlockSpec((d, tn), lambda i, j: (0, j)),
            pl.BlockSpec((d, tn), lambda i, j: (0, j + nj)),
            pl.BlockSpec((1, tn), lambda i, j: (0, j)),
        ],
        out_specs=(blk, blk, blk, blk),
        scratch_shapes=[pltpu.VMEM((tm, d), BF16)],
        compiler_params=_params("parallel", "arbitrary"),
        name="shared_kv",
    )(x, gain.reshape(1, d), w_kv, w_kv, k_norm.reshape(1, n))


def _sbq_kernel(x_ref, g_ref, w_ref, qn_ref, o_ref, h_ref):
    @pl.when(pl.program_id(1) == 0)
    def _():
        h_ref[...] = _rmsnorm_rows(x_ref[...], g_ref[...]).astype(BF16)

    y = _dot(h_ref[...], w_ref[...])
    for c, blk in enumerate(_head_rmsnorm(y, qn_ref[...], SB_HD)):
        o_ref[:, c * SB_HD:(c + 1) * SB_HD] = blk.astype(o_ref.dtype)


def _sb_query(x, gain, w_q, q_norm, *, tm, tn, out_dtype):
    m, d = x.shape
    n = w_q.shape[1]
    assert m % tm == 0 and n % tn == 0 and tn % SB_HD == 0
    return pl.pallas_call(
        _sbq_kernel,
        out_shape=jax.ShapeDtypeStruct((m, n), out_dtype),
        grid=(m // tm, n // tn),
        in_specs=[
            pl.BlockSpec((tm, d), lambda i, j: (i, 0)),
            pl.BlockSpec((1, d), lambda i, j: (0, 0)),
            pl.BlockSpec((d, tn), lambda i, j: (0, j)),
            pl.BlockSpec((1, tn), lambda i, j: (0, j)),
        ],
        out_specs=pl.BlockSpec((tm, tn), lambda i, j: (i, j)),
        scratch_shapes=[pltpu.VMEM((tm, d), BF16)],
        compiler_params=_params("parallel", "arbitrary"),
        name="sb_query",
    )(x, gain.reshape(1, d), w_q, q_norm.reshape(1, n))


def _suffix_matrix(tk, with_totals):
    width = tk + LANES if with_totals else tk
    j = lax.broadcasted_iota(jnp.int32, (tk, width), 0)
    s = lax.broadcasted_iota(jnp.int32, (tk, width), 1)
    return jnp.where((j > s) | (s >= tk), 1.0, 0.0).astype(BF16)


def _split_bf16(x):
    hi = x.astype(BF16)
    return hi, (x - hi.astype(F32)).astype(BF16)


def _sb_weights(qks, biases, carries, suffix, masks, chained, split):
    tq, tk = qks[0].shape
    log_betas, first_cols, parts = [], [], []
    for qk, bias, mask in zip(qks, biases, masks):
        z = qk * (SB_HD ** -0.5) + bias
        t = jnp.log(1.0 + jnp.exp2(jnp.abs(z) * -LOG2_E))
        log_beta = jnp.minimum(z, 0.0) - t
        log_betas.append(log_beta)
        log_keep = log_beta - z
        if mask is not None:
            log_keep = jnp.where(mask, log_keep, 0.0)
        first_cols.append(log_keep[:, 0:1])
        parts.extend(_split_bf16(log_keep) if split else [log_keep.astype(BF16)])
    r_all = _dot(jnp.concatenate(parts, axis=0), suffix)
    n_parts = 2 if split else 1
    weights, new_carries = [], []
    carry = carries if chained else None
    for i, (log_beta, mask) in enumerate(zip(log_betas, masks)):
        r = r_all[n_parts * i * tq:(n_parts * i + 1) * tq]
        if split:
            r = r + r_all[(2 * i + 1) * tq:(2 * i + 2) * tq]
        if not chained:
            carry = carries[i]
        carry_w = carry if tk == LANES else jnp.concatenate([carry] * (tk // LANES), axis=1)
        a = jnp.exp(log_beta + r[:, :tk] + carry_w)
        if mask is not None:
            a = jnp.where(mask, a, 0.0)
        weights.append(a.astype(BF16))
        if suffix.shape[1] > tk:
            tile_total = r[:, tk:]
        else:
            tile_total = jnp.broadcast_to(r[:, 0:1] + first_cols[i], (tq, LANES))
        carry = carry + tile_total
        new_carries.append(carry)
    return weights, (carry if chained else new_carries)


def _sb_prompt_kernel(bias_ref, q_ref, k_ref, v_ref, o_ref, acc_ref, carry_ref, *, tq, heads):
    hg = pl.program_id(1)
    i = pl.program_id(2)
    suffix = _suffix_matrix(tq, with_totals=False)
    row = lax.broadcasted_iota(jnp.int32, (tq, tq), 0)
    col = lax.broadcasted_iota(jnp.int32, (tq, tq), 1)
    diag_mask = col < row

    def tile(j, mask, first):
        start = pl.multiple_of(j * tq, tq)
        cols = [slice(hh * SB_HD, (hh + 1) * SB_HD) for hh in range(heads)]
        qks = [_dot_nt(q_ref[:, cs], k_ref[pl.ds(start, tq), cs]) for cs in cols]
        biases = [bias_ref[hg * heads + hh] for hh in range(heads)]
        carries = [jnp.zeros((tq, LANES), F32) if first else carry_ref[hh] for hh in range(heads)]
        weights, carries = _sb_weights(qks, biases, carries, suffix, [mask] * heads, chained=False, split=False)
        for hh, cs in enumerate(cols):
            av = _dot(weights[hh], v_ref[pl.ds(start, tq), cs])
            carry_ref[hh] = carries[hh]
            if first:
                acc_ref[hh] = av
            else:
                acc_ref[hh] += av

    tile(i, diag_mask, True)

    def body(t, _):
        tile(i - 1 - t, None, False)
        return 0

    lax.fori_loop(0, i, body, 0)
    for hh in range(heads):
        o_ref[:, hh * SB_HD:(hh + 1) * SB_HD] = acc_ref[hh].astype(o_ref.dtype)


def _sb_prompt(q, k, v, bias, *, batch, seq, tq, heads):
    m, n = q.shape
    nq = seq // tq
    w = heads * SB_HD
    assert seq % tq == 0 and n % w == 0 and tq % LANES == 0
    kern = functools.partial(_sb_prompt_kernel, tq=tq, heads=heads)
    return pl.pallas_call(
        kern,
        out_shape=jax.ShapeDtypeStruct((m, n), BF16),
        grid=(batch, n // w, nq),
        in_specs=[
            pl.BlockSpec(memory_space=pltpu.SMEM),
            pl.BlockSpec((tq, w), lambda b, h, i: (b * nq + i, h)),
            pl.BlockSpec((seq, w), lambda b, h, i: (b, h)),
            pl.BlockSpec((seq, w), lambda b, h, i: (b, h)),
        ],
        out_specs=pl.BlockSpec((tq, w), lambda b, h, i: (b * nq + i, h)),
        scratch_shapes=[pltpu.VMEM((heads, tq, SB_HD), F32), pltpu.VMEM((heads, tq, LANES), F32)],
        compiler_params=_params("parallel", "parallel", "arbitrary"),
        name="sb_prompt",
    )(bias, q, k, v)


def _sb_sample_kernel(pt_ref, q_ref, kn_ref, vn_ref, bias_ref, own_ref, fold_ref, spread_ref, *refs,
                      c, pages_per_step, page):
    k_refs = refs[:pages_per_step]
    v_refs = refs[pages_per_step:2 * pages_per_step]
    o_ref = refs[2 * pages_per_step]
    kpad_ref, vpad_ref, acc_ref, carry_ref = refs[2 * pages_per_step + 1:]
    jj = pl.program_id(1)
    groups = SB_HEADS // SB_HEAD_GROUP
    rows = SB_HEADS * c
    grows = SB_HEAD_GROUP * c
    prow = page * SB_HEAD_GROUP
    suffix = _suffix_matrix(page, with_totals=True)
    bias = bias_ref[...]
    own = own_ref[...]
    q = q_ref[0].astype(BF16)
    q_groups = [q[g * grows:(g + 1) * grows] for g in range(groups)]

    def page_groups(ref):
        return [ref[:, g].reshape(prow, SB_HD).astype(BF16) for g in range(groups)]

    def tiles(pks, pvs, carry, masks):
        n = len(pks)
        parts = []
        for pk in pks:
            for g in range(groups):
                parts.extend(_split_bf16(_dot_nt(q_groups[g], pk[g]) * own))
        folded = _dot(jnp.concatenate(parts, axis=0), fold_ref[...])
        qks = []
        for i in range(n):
            halves = []
            for g in range(groups):
                base = (i * groups + g) * 2 * grows
                halves.append(folded[base:base + grows] + folded[base + grows:base + 2 * grows])
            qks.append(jnp.concatenate(halves, axis=0))
        weights, carry = _sb_weights(qks, [bias] * n, carry, suffix, masks, chained=True, split=True)
        spread = _dot(jnp.concatenate(weights, axis=0), spread_ref[...])
        av = None
        for i, pv in enumerate(pvs):
            outs = []
            for g in range(groups):
                base = i * rows + g * grows
                a_wide = (spread[base:base + grows] * own).astype(BF16)
                outs.append(_dot(a_wide, pv[g]))
            term = jnp.concatenate(outs, axis=0)
            av = term if av is None else av + term
        return av, carry

    @pl.when(jj == 0)
    def _():
        kpad_ref[...] = jnp.zeros_like(kpad_ref)
        vpad_ref[...] = jnp.zeros_like(vpad_ref)
        kpad_ref[0:c] = kn_ref[0]
        vpad_ref[0:c] = vn_ref[0]
        r_idx = lax.broadcasted_iota(jnp.int32, (rows, page), 0)
        s_idx = lax.broadcasted_iota(jnp.int32, (rows, page), 1)
        av, carry = tiles([page_groups(kpad_ref)], [page_groups(vpad_ref)],
                          jnp.zeros((rows, LANES), F32), [s_idx < (r_idx % c)])
        acc_ref[...] = av
        carry_ref[...] = carry

    av, carry = tiles([page_groups(r.at[0]) for r in k_refs], [page_groups(r.at[0]) for r in v_refs],
                      carry_ref[...], [None] * pages_per_step)
    carry_ref[...] = carry
    acc_ref[...] += av

    @pl.when(jj == pl.num_programs(1) - 1)
    def _():
        o_ref[0] = acc_ref[...]


def _sb_sample(q, k_new, v_new, cache_k, cache_v, page_table, bias, *, pages_per_step):
    batch, rows, hd = q.shape
    c = rows // SB_HEADS
    n_pages = page_table.shape[1]
    page, groups, gh = cache_k.shape[1:4]
    prow = page * gh
    grows = gh * c
    steps = n_pages // pages_per_step
    assert n_pages % pages_per_step == 0 and page % LANES == 0 and gh == SB_HEAD_GROUP
    bias_rows = jnp.broadcast_to(jnp.repeat(bias.astype(F32), c)[:, None], (rows, page))
    own = (jnp.arange(prow)[None, :] % gh == jnp.arange(grows)[:, None] // c).astype(F32)
    fold = (jnp.arange(prow)[:, None] // gh == jnp.arange(page)[None, :]).astype(BF16)
    spread = fold.T

    def page_spec(p):
        return pl.BlockSpec(
            (1, page, groups, gh, hd),
            lambda b, jj, pt: (pt[b * n_pages + n_pages - 1 - (jj * pages_per_step + p)], 0, 0, 0, 0))

    const = lambda shape: pl.BlockSpec(shape, lambda b, jj, pt: (0,) * len(shape))
    new_spec = pl.BlockSpec((1, c, groups, gh, hd), lambda b, jj, pt: (b, 0, 0, 0, 0))
    kern = functools.partial(_sb_sample_kernel, c=c, pages_per_step=pages_per_step, page=page)
    grid_spec = pltpu.PrefetchScalarGridSpec(
        num_scalar_prefetch=1,
        grid=(batch, steps),
        in_specs=[pl.BlockSpec((1, rows, hd), lambda b, jj, pt: (b, 0, 0)), new_spec, new_spec,
                  const((rows, page)), const((grows, prow)), const((prow, page)), const((page, prow))]
        + [page_spec(p) for p in range(pages_per_step)] * 2,
        out_specs=pl.BlockSpec((1, rows, hd), lambda b, jj, pt: (b, 0, 0)),
        scratch_shapes=[pltpu.VMEM((page, groups, gh, hd), F32), pltpu.VMEM((page, groups, gh, hd), F32),
                        pltpu.VMEM((rows, hd), F32), pltpu.VMEM((rows, LANES), F32)],
    )
    return pl.pallas_call(
        kern,
        out_shape=jax.ShapeDtypeStruct((batch, rows, hd), F32),
        grid_spec=grid_spec,
        compiler_params=_params("parallel", "arbitrary"),
        name="sb_sample",
    )(page_table.reshape(-1), q, k_new, v_new, bias_rows, own, fold, spread,
      *([cache_k] * pages_per_step), *([cache_v] * pages_per_step))


def _rope_tables(pos, half):
    inv = ROPE_BASE ** (-jnp.arange(half, dtype=F32) / half)
    ang = pos[:, None] * inv[None, :]
    return jnp.cos(ang), jnp.sin(ang)


def kernel(x_prompt, x_sample, state_ret, cache_k, cache_v, page_table, ffn1_norm, ffn1_w_in, ffn1_w_out, mix_norm, ffn2_norm, ffn2_w_in, ffn2_w_out, ret_w_in, ret_gn, ret_w_out, kv_norm, w_kv, k_norm, sb_w_q, sb_q_norm, sb_bias, sb_w_out):
    batch, seq, d = x_prompt.shape
    dec_b, dec_s, _ = x_sample.shape
    n_pool, page = cache_k.shape[:2]
    past_len = page_table.shape[1] * page
    depth = ffn1_norm.shape[0]
    n_a = ret_w_in.shape[0]
    dk = d // RET_HEADS
    mp, ms = batch * seq, dec_b * dec_s

    bf = lambda w: w.astype(BF16)
    w_kv = bf(w_kv)
    ffn_w = ((bf(ffn1_w_in), bf(ffn1_w_out * FFN_RES)), (bf(ffn2_w_in), bf(ffn2_w_out * FFN_RES)))

    tm_p, tm_s = 512, ms
    tm_big = min(1024, mp)
    cos_p, sin_p = _rope_tables(jnp.arange(seq, dtype=F32), dk // 2)
    cos_s, sin_s = _rope_tables(past_len + jnp.arange(dec_s, dtype=F32), dk // 2)
    cos_s, sin_s = jnp.tile(cos_s, (dec_b, 1)), jnp.tile(sin_s, (dec_b, 1))

    xp = x_prompt.reshape(mp, d)
    xs = x_sample.reshape(ms, d)
    grouped = (SB_HEADS // SB_HEAD_GROUP, SB_HEAD_GROUP, SB_HD)
    cache_k2 = cache_k.reshape(n_pool, page, *grouped)
    cache_v2 = cache_v.reshape(n_pool, page, *grouped)

    def ffn_pair(xp, xs, norm, which, layer):
        w_in, w_out = ffn_w[which]
        return (_ffn(xp, norm, w_in, w_out, layer, tm=tm_big, tf=512),
                _ffn(xs, norm, w_in, w_out, layer, tm=tm_s, tf=512))

    ret_p, ret_s = [], []
    k_p = v_p = k_s = v_s = kb_p = vb_p = None
    for layer in range(depth):
        if layer == n_a:
            k_p, v_p, kb_p, vb_p = _shared_kv(xp, kv_norm, w_kv, k_norm, tm=tm_big, tn=512)
            k_s, v_s, _, _ = _shared_kv(xs, kv_norm, w_kv, k_norm, tm=tm_s, tn=512)

        xp, xs = ffn_pair(xp, xs, ffn1_norm[layer], 0, layer)

        if layer < n_a:
            a = layer
            w_in, w_out = bf(ret_w_in[a]), bf(ret_w_out[a])
            proj_p = _ret_in(xp, mix_norm[layer], w_in, cos_p, sin_p, tm=tm_big, tn=1024, out_dtype=BF16)
            proj_s = _ret_in(xs, mix_norm[layer], w_in, cos_s, sin_s, tm=tm_s, tn=1024, out_dtype=F32)
            og_p, st_p = _ret_prompt(proj_p, ret_gn[a], batch=batch, seq=seq)
            og_s, st_s = _ret_sample(proj_s.reshape(dec_b, dec_s, -1), state_ret[a], ret_gn[a])
            ret_p.append(st_p)
            ret_s.append(st_s)
            xp = _matmul_res(og_p, w_out, xp, tm=tm_big, tn=512)
            xs = _matmul_res(og_s.reshape(ms, -1), w_out, xs, tm=tm_s, tn=512)
        else:
            bl = layer - n_a
            w_q, w_out = bf(sb_w_q[bl]), bf(sb_w_out[bl])
            q_p = _sb_query(xp, mix_norm[layer], w_q, sb_q_norm[bl], tm=tm_big, tn=2048, out_dtype=BF16)
            q_s = _sb_query(xs, mix_norm[layer], w_q, sb_q_norm[bl], tm=tm_s, tn=1024, out_dtype=F32)
            at_p = _sb_prompt(q_p, kb_p, vb_p, sb_bias[bl], batch=batch, seq=seq, tq=256, heads=16)
            q_hq = q_s.reshape(dec_b, dec_s, SB_HEADS, SB_HD).transpose(0, 2, 1, 3)
            at_hq = _sb_sample(q_hq.reshape(dec_b, SB_HEADS * dec_s, SB_HD),
                               k_s.reshape(dec_b, dec_s, *grouped), v_s.reshape(dec_b, dec_s, *grouped),
                               cache_k2, cache_v2, page_table, sb_bias[bl], pages_per_step=8)
            at_s = at_hq.reshape(dec_b, SB_HEADS, dec_s, SB_HD).transpose(0, 2, 1, 3).reshape(ms, d)
            xp = _matmul_res(at_p, w_out, xp, tm=tm_big, tn=1024)
            xs = _matmul_res(at_s, w_out, xs, tm=tm_s, tn=1024)

        xp, xs = ffn_pair(xp, xs, ffn2_norm[layer], 1, layer)

    hd_shape = (SB_HEADS, SB_HD)
    return (xp.reshape(batch, seq, d), xs.reshape(dec_b, dec_s, d),
            jnp.stack(ret_p, axis=0), jnp.stack(ret_s, axis=0),
            k_p.reshape(batch, seq, *hd_shape), v_p.reshape(batch, seq, *hd_shape),
            k_s.reshape(dec_b, dec_s, *hd_shape), v_s.reshape(dec_b, dec_s, *hd_shape))
```

```python
import functools
import math

import jax
import jax.numpy as jnp
from jax import lax
from jax.experimental import pallas as pl
from jax.experimental.pallas import tpu as pltpu

F32 = jnp.float32
BF16 = jnp.bfloat16

EPS = 1e-6
LOG2_E = 1.0 / math.log(2.0)
FFN_RES = 0.5
ROPE_BASE = 10000.0
RET_HEADS = 8
RET_CHUNK = 128
SB_HEADS = 16
SB_HD = 128

LANES = 128
SUBLANES = 8
VMEM_LIMIT_BYTES = 58 * 1024 * 1024

SB_HEAD_GROUP = SUBLANES


def _params(*semantics):
    return pltpu.CompilerParams(dimension_semantics=semantics, vmem_limit_bytes=VMEM_LIMIT_BYTES)


def _rmsnorm_rows(x, gain):
    return x * lax.rsqrt(jnp.mean(x * x, axis=-1, keepdims=True) + EPS) * gain


def _head_rmsnorm(y, gain, hd):
    outs = []
    for c in range(y.shape[1] // hd):
        blk = y[:, c * hd:(c + 1) * hd]
        outs.append(blk * lax.rsqrt(jnp.mean(blk * blk, axis=-1, keepdims=True) + EPS)
                    * gain[:, c * hd:(c + 1) * hd])
    return outs


def _dot(a, b):
    return jnp.dot(a, b, preferred_element_type=F32)


def _dot_nt(a, b):
    return lax.dot_general(a, b, (((1,), (1,)), ((), ())), preferred_element_type=F32)


def _dot_tn(a, b):
    return lax.dot_general(a, b, (((0,), (0,)), ((), ())), preferred_element_type=F32)


def _ffn_kernel(x_ref, g_ref, wg_ref, wu_ref, wo_ref, o_ref, h_ref):
    j = pl.program_id(1)

    @pl.when(j == 0)
    def _():
        x = x_ref[...]
        h_ref[...] = _rmsnorm_rows(x, g_ref[...]).astype(BF16)
        o_ref[...] = x

    h = h_ref[...]
    gate = _dot(h, wg_ref[...])
    up = _dot(h, wu_ref[...])
    act = (gate * jax.nn.sigmoid(gate) * up).astype(BF16)
    o_ref[...] += _dot(act, wo_ref[...])


def _ffn(x, gain, w_in, w_out_scaled, layer, *, tm, tf):
    m, d = x.shape
    dff = w_out_scaled.shape[1]
    nj = dff // tf
    assert m % tm == 0 and dff % tf == 0
    return pl.pallas_call(
        _ffn_kernel,
        out_shape=jax.ShapeDtypeStruct((m, d), F32),
        grid=(m // tm, nj),
        in_specs=[
            pl.BlockSpec((tm, d), lambda i, j: (i, 0)),
            pl.BlockSpec((1, d), lambda i, j: (0, 0)),
            pl.BlockSpec((None, d, tf), lambda i, j: (layer, 0, j)),
            pl.BlockSpec((None, d, tf), lambda i, j: (layer, 0, j + nj)),
            pl.BlockSpec((None, tf, d), lambda i, j: (layer, j, 0)),
        ],
        out_specs=pl.BlockSpec((tm, d), lambda i, j: (i, 0)),
        scratch_shapes=[pltpu.VMEM((tm, d), BF16)],
        compiler_params=_params("parallel", "arbitrary"),
        name="ffn",
    )(x, gain.reshape(1, d), w_in, w_in, w_out_scaled)


def _ret_in_kernel(x_ref, g_ref, w_ref, cos_ref, sin_ref, o_ref, h_ref, *, nq, nk, dk, k_scale):
    j = pl.program_id(1)

    @pl.when(j == 0)
    def _():
        h_ref[...] = _rmsnorm_rows(x_ref[...], g_ref[...]).astype(BF16)

    y = _dot(h_ref[...], w_ref[...])
    half = dk // 2

    @pl.when(j < nq + nk)
    def _():
        sc = jnp.where(j < nq, 1.0, k_scale).astype(F32)
        cos = cos_ref[...]
        sin = sin_ref[...]
        for hh in range(y.shape[1] // dk):
            x1 = y[:, hh * dk:hh * dk + half]
            x2 = y[:, hh * dk + half:(hh + 1) * dk]
            o_ref[:, hh * dk:hh * dk + half] = ((x1 * cos - x2 * sin) * sc).astype(o_ref.dtype)
            o_ref[:, hh * dk + half:(hh + 1) * dk] = ((x1 * sin + x2 * cos) * sc).astype(o_ref.dtype)

    @pl.when(j >= nq + nk)
    def _():
        o_ref[...] = y.astype(o_ref.dtype)


def _ret_in(x, gain, w, cos, sin, *, tm, tn, out_dtype):
    m, d = x.shape
    n = w.shape[1]
    dk = d // RET_HEADS
    nper = cos.shape[0] // tm
    assert m % tm == 0 and n % tn == 0 and tn % dk == 0 and cos.shape[0] % tm == 0
    kern = functools.partial(_ret_in_kernel, nq=d // tn, nk=d // tn, dk=dk, k_scale=dk ** -0.5)
    return pl.pallas_call(
        kern,
        out_shape=jax.ShapeDtypeStruct((m, n), out_dtype),
        grid=(m // tm, n // tn),
        in_specs=[
            pl.BlockSpec((tm, d), lambda i, j: (i, 0)),
            pl.BlockSpec((1, d), lambda i, j: (0, 0)),
            pl.BlockSpec((d, tn), lambda i, j: (0, j)),
            pl.BlockSpec((tm, dk // 2), lambda i, j: (i % nper, 0)),
            pl.BlockSpec((tm, dk // 2), lambda i, j: (i % nper, 0)),
        ],
        out_specs=pl.BlockSpec((tm, tn), lambda i, j: (i, j)),
        scratch_shapes=[pltpu.VMEM((tm, d), BF16)],
        compiler_params=_params("parallel", "arbitrary"),
        name="ret_in",
    )(x, gain.reshape(1, d), w, cos, sin)


def _retention_heads(q_ref, k_ref, v_ref, g_ref, gn_ref, og_ref, get_state, put_state, *, c, dk, dv):
    row = lax.broadcasted_iota(jnp.int32, (c, c), 0)
    col = lax.broadcasted_iota(jnp.int32, (c, c), 1)
    diff = (row - col).astype(F32)
    idx = lax.broadcasted_iota(jnp.int32, (c, 1), 0).astype(F32)
    for h in range(RET_HEADS):
        lg = math.log(1.0 - 2.0 ** (-5.0 - h))
        q = q_ref[:, h * dk:(h + 1) * dk]
        k = k_ref[:, h * dk:(h + 1) * dk]
        v = v_ref[:, h * dv:(h + 1) * dv].astype(BF16)
        state = get_state(h)
        decay = jnp.where(diff >= 0, jnp.exp(lg * jnp.maximum(diff, 0.0)), 0.0)
        scores = _dot_nt(q.astype(BF16), k.astype(BF16)) * decay
        inner = _dot(scores.astype(BF16), v)
        q_decay = jnp.exp((idx + 1.0) * lg)
        cross = _dot((q.astype(F32) * q_decay).astype(BF16), state.astype(BF16))
        k_decay = jnp.exp((c - 1.0 - idx) * lg)
        kd = (k.astype(F32) * k_decay).astype(BF16)
        put_state(h, math.exp(c * lg) * state + _dot_tn(kd, v))
        o = inner + cross
        on = o * lax.rsqrt(jnp.mean(o * o, axis=-1, keepdims=True) + EPS) * gn_ref[:, h * dv:(h + 1) * dv]
        g = g_ref[:, h * dv:(h + 1) * dv].astype(F32)
        og_ref[:, h * dv:(h + 1) * dv] = (on * (g * jax.nn.sigmoid(g))).astype(og_ref.dtype)


def _ret_prompt_kernel(q_ref, k_ref, v_ref, g_ref, gn_ref, og_ref, st_ref, *, c, dk, dv):
    @pl.when(pl.program_id(1) == 0)
    def _():
        st_ref[...] = jnp.zeros_like(st_ref)

    def get_state(h):
        return st_ref[0, h]

    def put_state(h, s):
        st_ref[0, h] = s

    _retention_heads(q_ref, k_ref, v_ref, g_ref, gn_ref, og_ref, get_state, put_state, c=c, dk=dk, dv=dv)


def _ret_prompt(proj, gn, *, batch, seq):
    m, n = proj.shape
    d = n // 6
    dk, dv, c = d // RET_HEADS, 2 * d // RET_HEADS, RET_CHUNK
    nc = seq // c
    kern = functools.partial(_ret_prompt_kernel, c=c, dk=dk, dv=dv)
    og, st = pl.pallas_call(
        kern,
        out_shape=(jax.ShapeDtypeStruct((m, 2 * d), BF16),
                   jax.ShapeDtypeStruct((batch, RET_HEADS, dk, dv), F32)),
        grid=(batch, nc),
        in_specs=[
            pl.BlockSpec((c, d), lambda b, t: (b * nc + t, 0)),
            pl.BlockSpec((c, d), lambda b, t: (b * nc + t, 1)),
            pl.BlockSpec((c, 2 * d), lambda b, t: (b * nc + t, 1)),
            pl.BlockSpec((c, 2 * d), lambda b, t: (b * nc + t, 2)),
            pl.BlockSpec((1, 2 * d), lambda b, t: (0, 0)),
        ],
        out_specs=(pl.BlockSpec((c, 2 * d), lambda b, t: (b * nc + t, 0)),
                   pl.BlockSpec((1, RET_HEADS, dk, dv), lambda b, t: (b, 0, 0, 0))),
        compiler_params=_params("parallel", "arbitrary"),
        name="ret_prompt",
    )(proj, proj, proj, proj, gn.reshape(1, 2 * d))
    return og, st


def _ret_sample_kernel(q_ref, k_ref, v_ref, g_ref, gn_ref, st_in_ref, og_ref, st_ref, *, c, dk, dv):
    def get_state(h):
        return st_in_ref[0, h]

    def put_state(h, s):
        st_ref[0, h] = s

    _retention_heads(q_ref.at[0], k_ref.at[0], v_ref.at[0], g_ref.at[0], gn_ref, og_ref.at[0],
                     get_state, put_state, c=c, dk=dk, dv=dv)


def _ret_sample(proj, state, gn):
    batch, c, n = proj.shape
    d = n // 6
    dk, dv = d // RET_HEADS, 2 * d // RET_HEADS
    kern = functools.partial(_ret_sample_kernel, c=c, dk=dk, dv=dv)
    og, st = pl.pallas_call(
        kern,
        out_shape=(jax.ShapeDtypeStruct((batch, c, 2 * d), F32),
                   jax.ShapeDtypeStruct((batch, RET_HEADS, dk, dv), F32)),
        grid=(batch,),
        in_specs=[
            pl.BlockSpec((1, c, d), lambda b: (b, 0, 0)),
            pl.BlockSpec((1, c, d), lambda b: (b, 0, 1)),
            pl.BlockSpec((1, c, 2 * d), lambda b: (b, 0, 1)),
            pl.BlockSpec((1, c, 2 * d), lambda b: (b, 0, 2)),
            pl.BlockSpec((1, 2 * d), lambda b: (0, 0)),
            pl.BlockSpec((1, RET_HEADS, dk, dv), lambda b: (b, 0, 0, 0)),
        ],
        out_specs=(pl.BlockSpec((1, c, 2 * d), lambda b: (b, 0, 0)),
                   pl.BlockSpec((1, RET_HEADS, dk, dv), lambda b: (b, 0, 0, 0))),
        compiler_params=_params("parallel"),
        name="ret_sample",
    )(proj, proj, proj, proj, gn.reshape(1, 2 * d), state)
    return og, st


def _matmul_res_kernel(a_ref, w_ref, r_ref, o_ref):
    o_ref[...] = r_ref[...] + _dot(a_ref[...].astype(BF16), w_ref[...])


def _matmul_res(a, w, res, *, tm, tn):
    m, k = a.shape
    n = w.shape[1]
    assert m % tm == 0 and n % tn == 0
    return pl.pallas_call(
        _matmul_res_kernel,
        out_shape=jax.ShapeDtypeStruct((m, n), F32),
        grid=(m // tm, n // tn),
        in_specs=[
            pl.BlockSpec((tm, k), lambda i, j: (i, 0)),
            pl.BlockSpec((k, tn), lambda i, j: (0, j)),
            pl.BlockSpec((tm, tn), lambda i, j: (i, j)),
        ],
        out_specs=pl.BlockSpec((tm, tn), lambda i, j: (i, j)),
        compiler_params=_params("parallel", "arbitrary"),
        name="matmul_res",
    )(a, w, res)


def _kv_kernel(x_ref, g_ref, wk_ref, wv_ref, kn_ref, k_ref, v_ref, kb_ref, vb_ref, h_ref):
    @pl.when(pl.program_id(1) == 0)
    def _():
        h_ref[...] = _rmsnorm_rows(x_ref[...], g_ref[...]).astype(BF16)

    h = h_ref[...]
    yk = _dot(h, wk_ref[...])
    for c, blk in enumerate(_head_rmsnorm(yk, kn_ref[...], SB_HD)):
        k_ref[:, c * SB_HD:(c + 1) * SB_HD] = blk
        kb_ref[:, c * SB_HD:(c + 1) * SB_HD] = blk.astype(BF16)
    yv = _dot(h, wv_ref[...])
    v_ref[...] = yv
    vb_ref[...] = yv.astype(BF16)


def _shared_kv(x, gain, w_kv, k_norm, *, tm, tn):
    m, d = x.shape
    n = w_kv.shape[1] // 2
    nj = n // tn
    assert m % tm == 0 and n % tn == 0 and tn % SB_HD == 0
    blk = pl.BlockSpec((tm, tn), lambda i, j: (i, j))
    return pl.pallas_call(
        _kv_kernel,
        out_shape=(jax.ShapeDtypeStruct((m, n), F32), jax.ShapeDtypeStruct((m, n), F32),
                   jax.ShapeDtypeStruct((m, n), BF16), jax.ShapeDtypeStruct((m, n), BF16)),
        grid=(m // tm, nj),
        in_specs=[
            pl.BlockSpec((tm, d), lambda i, j: (i, 0)),
            pl.BlockSpec((1, d), lambda i, j: (0, 0)),
            pl.BlockSpec((d, tn), lambda i, j: (0, j)),
            pl.BlockSpec((d, tn), lambda i, j: (0, j + nj)),
            pl.BlockSpec((1, tn), lambda i, j: (0, j)),
        ],
        out_specs=(blk, blk, blk, blk),
        scratch_shapes=[pltpu.VMEM((tm, d), BF16)],
        compiler_params=_params("parallel", "arbitrary"),
        name="shared_kv",
    )(x, gain.reshape(1, d), w_kv, w_kv, k_norm.reshape(1, n))


def _sbq_kernel(x_ref, g_ref, w_ref, qn_ref, o_ref, h_ref):
    @pl.when(pl.program_id(1) == 0)
    def _():
        h_ref[...] = _rmsnorm_rows(x_ref[...], g_ref[...]).astype(BF16)

    y = _dot(h_ref[...], w_ref[...])
    for c, blk in enumerate(_head_rmsnorm(y, qn_ref[...], SB_HD)):
        o_ref[:, c * SB_HD:(c + 1) * SB_HD] = (blk * (SB_HD ** -0.5)).astype(o_ref.dtype)


def _sb_query(x, gain, w_q, q_norm, *, tm, tn, out_dtype):
    m, d = x.shape
    n = w_q.shape[1]
    assert m % tm == 0 and n % tn == 0 and tn % SB_HD == 0
    return pl.pallas_call(
        _sbq_kernel,
        out_shape=jax.ShapeDtypeStruct((m, n), out_dtype),
        grid=(m // tm, n // tn),
        in_specs=[
            pl.BlockSpec((tm, d), lambda i, j: (i, 0)),
            pl.BlockSpec((1, d), lambda i, j: (0, 0)),
            pl.BlockSpec((d, tn), lambda i, j: (0, j)),
            pl.BlockSpec((1, tn), lambda i, j: (0, j)),
        ],
        out_specs=pl.BlockSpec((tm, tn), lambda i, j: (i, j)),
        scratch_shapes=[pltpu.VMEM((tm, d), BF16)],
        compiler_params=_params("parallel", "arbitrary"),
        name="sb_query",
    )(x, gain.reshape(1, d), w_q, q_norm.reshape(1, n))


def _suffix_matrix(tk, with_totals):
    width = tk + LANES if with_totals else tk
    j = lax.broadcasted_iota(jnp.int32, (tk, width), 0)
    s = lax.broadcasted_iota(jnp.int32, (tk, width), 1)
    return jnp.where((j > s) | (s >= tk), 1.0, 0.0).astype(BF16)


def _split_bf16(x):
    hi = x.astype(BF16)
    return hi, (x - hi.astype(F32)).astype(BF16)


def _sb_weights(qks, biases, carries, suffix, masks, chained, split):
    tq, tk = qks[0].shape
    log_betas, first_cols, parts = [], [], []
    for qk, bias, mask in zip(qks, biases, masks):
        z = qk + bias
        t = jnp.log(1.0 + jnp.exp2(jnp.abs(z) * -LOG2_E))
        log_beta = jnp.minimum(z, 0.0) - t
        log_betas.append(log_beta)
        log_keep = log_beta - z
        if mask is not None:
            log_keep = jnp.where(mask, log_keep, 0.0)
        first_cols.append(log_keep[:, 0:1])
        parts.extend(_split_bf16(log_keep) if split else [log_keep.astype(BF16)])
    r_all = _dot(jnp.concatenate(parts, axis=0), suffix)
    n_parts = 2 if split else 1
    weights, new_carries = [], []
    carry = carries if chained else None
    for i, (log_beta, mask) in enumerate(zip(log_betas, masks)):
        r = r_all[n_parts * i * tq:(n_parts * i + 1) * tq]
        if split:
            r = r + r_all[(2 * i + 1) * tq:(2 * i + 2) * tq]
        if not chained:
            carry = carries[i]
        carry_w = carry if tk == LANES else jnp.concatenate([carry] * (tk // LANES), axis=1)
        a = jnp.exp(log_beta + r[:, :tk] + carry_w)
        if mask is not None:
            a = jnp.where(mask, a, 0.0)
        weights.append(a.astype(BF16))
        if suffix.shape[1] > tk:
            tile_total = r[:, tk:]
        else:
            tile_total = jnp.broadcast_to(r[:, 0:1] + first_cols[i], (tq, LANES))
        carry = carry + tile_total
        new_carries.append(carry)
    return weights, (carry if chained else new_carries)


def _sb_prompt_kernel(bias_ref, q_ref, k_ref, v_ref, o_ref, acc_ref, carry_ref, *, tq, heads):
    hg = pl.program_id(1)
    i = pl.program_id(2)
    suffix = _suffix_matrix(tq, with_totals=False)
    row = lax.broadcasted_iota(jnp.int32, (tq, tq), 0)
    col = lax.broadcasted_iota(jnp.int32, (tq, tq), 1)
    diag_mask = col < row

    def tile(j, mask, first):
        start = pl.multiple_of(j * tq, tq)
        cols = [slice(hh * SB_HD, (hh + 1) * SB_HD) for hh in range(heads)]
        qks = [_dot_nt(q_ref[:, cs], k_ref[pl.ds(start, tq), cs]) for cs in cols]
        biases = [bias_ref[hg * heads + hh] for hh in range(heads)]
        carries = [jnp.zeros((tq, LANES), F32) if first else carry_ref[hh] for hh in range(heads)]
        weights, carries = _sb_weights(qks, biases, carries, suffix, [mask] * heads, chained=False, split=False)
        for hh, cs in enumerate(cols):
            av = _dot(weights[hh], v_ref[pl.ds(start, tq), cs])
            carry_ref[hh] = carries[hh]
            if first:
                acc_ref[hh] = av
            else:
                acc_ref[hh] += av

    tile(i, diag_mask, True)

    def body(t, _):
        tile(i - 1 - t, None, False)
        return 0

    lax.fori_loop(0, i, body, 0)
    for hh in range(heads):
        o_ref[:, hh * SB_HD:(hh + 1) * SB_HD] = acc_ref[hh].astype(o_ref.dtype)


def _sb_prompt(q, k, v, bias, *, batch, seq, tq, heads):
    m, n = q.shape
    nq = seq // tq
    w = heads * SB_HD
    assert seq % tq == 0 and n % w == 0 and tq % LANES == 0
    kern = functools.partial(_sb_prompt_kernel, tq=tq, heads=heads)
    return pl.pallas_call(
        kern,
        out_shape=jax.ShapeDtypeStruct((m, n), BF16),
        grid=(batch, n // w, nq),
        in_specs=[
            pl.BlockSpec(memory_space=pltpu.SMEM),
            pl.BlockSpec((tq, w), lambda b, h, i: (b * nq + i, h)),
            pl.BlockSpec((seq, w), lambda b, h, i: (b, h)),
            pl.BlockSpec((seq, w), lambda b, h, i: (b, h)),
        ],
        out_specs=pl.BlockSpec((tq, w), lambda b, h, i: (b * nq + i, h)),
        scratch_shapes=[pltpu.VMEM((heads, tq, SB_HD), F32), pltpu.VMEM((heads, tq, LANES), F32)],
        compiler_params=_params("parallel", "parallel", "arbitrary"),
        name="sb_prompt",
    )(bias, q, k, v)


def _sb_sample_kernel(pt_ref, q_ref, kn_ref, vn_ref, bias_ref, own_ref, fold_ref, spread_ref, *refs,
                      c, pages_per_step, page):
    k_refs = refs[:pages_per_step]
    v_refs = refs[pages_per_step:2 * pages_per_step]
    o_ref = refs[2 * pages_per_step]
    kpad_ref, vpad_ref, acc_ref, carry_ref = refs[2 * pages_per_step + 1:]
    jj = pl.program_id(1)
    groups = SB_HEADS // SB_HEAD_GROUP
    rows = SB_HEADS * c
    grows = SB_HEAD_GROUP * c
    prow = page * SB_HEAD_GROUP
    suffix = _suffix_matrix(page, with_totals=True)
    bias = bias_ref[...]
    own = own_ref[...]
    q = q_ref[0].astype(BF16)
    q_groups = [q[g * grows:(g + 1) * grows] for g in range(groups)]

    def page_groups(ref):
        return [ref[:, g].reshape(prow, SB_HD).astype(BF16) for g in range(groups)]

    def tiles(pks, pvs, carry, masks):
        n = len(pks)
        parts = []
        for pk in pks:
            for g in range(groups):
                parts.extend(_split_bf16(_dot_nt(q_groups[g], pk[g]) * own))
        folded = _dot(jnp.concatenate(parts, axis=0), fold_ref[...])
        qks = []
        for i in range(n):
            halves = []
            for g in range(groups):
                base = (i * groups + g) * 2 * grows
                halves.append(folded[base:base + grows] + folded[base + grows:base + 2 * grows])
            qks.append(jnp.concatenate(halves, axis=0))
        weights, carry = _sb_weights(qks, [bias] * n, carry, suffix, masks, chained=True, split=True)
        spread = _dot(jnp.concatenate(weights, axis=0), spread_ref[...])
        av = None
        for i, pv in enumerate(pvs):
            outs = []
            for g in range(groups):
                base = i * rows + g * grows
                a_wide = (spread[base:base + grows] * own).astype(BF16)
                outs.append(_dot(a_wide, pv[g]))
            term = jnp.concatenate(outs, axis=0)
            av = term if av is None else av + term
        return av, carry

    @pl.when(jj == 0)
    def _():
        kpad_ref[...] = jnp.zeros_like(kpad_ref)
        vpad_ref[...] = jnp.zeros_like(vpad_ref)
        kpad_ref[0:c] = kn_ref[0]
        vpad_ref[0:c] = vn_ref[0]
        r_idx = lax.broadcasted_iota(jnp.int32, (rows, page), 0)
        s_idx = lax.broadcasted_iota(jnp.int32, (rows, page), 1)
        av, carry = tiles([page_groups(kpad_ref)], [page_groups(vpad_ref)],
                          jnp.zeros((rows, LANES), F32), [s_idx < (r_idx % c)])
        acc_ref[...] = av
        carry_ref[...] = carry

    av, carry = tiles([page_groups(r.at[0]) for r in k_refs], [page_groups(r.at[0]) for r in v_refs],
                      carry_ref[...], [None] * pages_per_step)
    carry_ref[...] = carry
    acc_ref[...] += av

    @pl.when(jj == pl.num_programs(1) - 1)
    def _():
        o_ref[0] = acc_ref[...]


def _sb_sample(q, k_new, v_new, cache_k, cache_v, page_table, bias, *, pages_per_step):
    batch, rows, hd = q.shape
    c = rows // SB_HEADS
    n_pages = page_table.shape[1]
    page, groups, gh = cache_k.shape[1:4]
    prow = page * gh
    grows = gh * c
    steps = n_pages // pages_per_step
    assert n_pages % pages_per_step == 0 and page % LANES == 0 and gh == SB_HEAD_GROUP
    bias_rows = jnp.broadcast_to(jnp.repeat(bias.astype(F32), c)[:, None], (rows, page))
    own = (jnp.arange(prow)[None, :] % gh == jnp.arange(grows)[:, None] // c).astype(F32)
    fold = (jnp.arange(prow)[:, None] // gh == jnp.arange(page)[None, :]).astype(BF16)
    spread = fold.T

    def page_spec(p):
        return pl.BlockSpec(
            (1, page, groups, gh, hd),
            lambda b, jj, pt: (pt[b * n_pages + n_pages - 1 - (jj * pages_per_step + p)], 0, 0, 0, 0))

    const = lambda shape: pl.BlockSpec(shape, lambda b, jj, pt: (0,) * len(shape))
    new_spec = pl.BlockSpec((1, c, groups, gh, hd), lambda b, jj, pt: (b, 0, 0, 0, 0))
    kern = functools.partial(_sb_sample_kernel, c=c, pages_per_step=pages_per_step, page=page)
    grid_spec = pltpu.PrefetchScalarGridSpec(
        num_scalar_prefetch=1,
        grid=(batch, steps),
        in_specs=[pl.BlockSpec((1, rows, hd), lambda b, jj, pt: (b, 0, 0)), new_spec, new_spec,
                  const((rows, page)), const((grows, prow)), const((prow, page)), const((page, prow))]
        + [page_spec(p) for p in range(pages_per_step)] * 2,
        out_specs=pl.BlockSpec((1, rows, hd), lambda b, jj, pt: (b, 0, 0)),
        scratch_shapes=[pltpu.VMEM((page, groups, gh, hd), F32), pltpu.VMEM((page, groups, gh, hd), F32),
                        pltpu.VMEM((rows, hd), F32), pltpu.VMEM((rows, LANES), F32)],
    )
    return pl.pallas_call(
        kern,
        out_shape=jax.ShapeDtypeStruct((batch, rows, hd), F32),
        grid_spec=grid_spec,
        compiler_params=_params("parallel", "arbitrary"),
        name="sb_sample",
    )(page_table.reshape(-1), q, k_new, v_new, bias_rows, own, fold, spread,
      *([cache_k] * pages_per_step), *([cache_v] * pages_per_step))


def _rope_tables(pos, half):
    inv = ROPE_BASE ** (-jnp.arange(half, dtype=F32) / half)
    ang = pos[:, None] * inv[None, :]
    return jnp.cos(ang), jnp.sin(ang)


def kernel(x_prompt, x_sample, state_ret, cache_k, cache_v, page_table, ffn1_norm, ffn1_w_in, ffn1_w_out, mix_norm, ffn2_norm, ffn2_w_in, ffn2_w_out, ret_w_in, ret_gn, ret_w_out, kv_norm, w_kv, k_norm, sb_w_q, sb_q_norm, sb_bias, sb_w_out):
    batch, seq, d = x_prompt.shape
    dec_b, dec_s, _ = x_sample.shape
    n_pool, page = cache_k.shape[:2]
    past_len = page_table.shape[1] * page
    depth = ffn1_norm.shape[0]
    n_a = ret_w_in.shape[0]
    dk = d // RET_HEADS
    mp, ms = batch * seq, dec_b * dec_s

    bf = lambda w: w.astype(BF16)
    w_kv = bf(w_kv)
    ffn_w = ((bf(ffn1_w_in), bf(ffn1_w_out * FFN_RES)), (bf(ffn2_w_in), bf(ffn2_w_out * FFN_RES)))

    tm_p, tm_s = 512, ms
    tm_big = min(1024, mp)
    cos_p, sin_p = _rope_tables(jnp.arange(seq, dtype=F32), dk // 2)
    cos_s, sin_s = _rope_tables(past_len + jnp.arange(dec_s, dtype=F32), dk // 2)
    cos_s, sin_s = jnp.tile(cos_s, (dec_b, 1)), jnp.tile(sin_s, (dec_b, 1))

    xp = x_prompt.reshape(mp, d)
    xs = x_sample.reshape(ms, d)
    grouped = (SB_HEADS // SB_HEAD_GROUP, SB_HEAD_GROUP, SB_HD)
    cache_k2 = cache_k.reshape(n_pool, page, *grouped)
    cache_v2 = cache_v.reshape(n_pool, page, *grouped)

    def ffn_pair(xp, xs, norm, which, layer):
        w_in, w_out = ffn_w[which]
        return (_ffn(xp, norm, w_in, w_out, layer, tm=tm_big, tf=512),
                _ffn(xs, norm, w_in, w_out, layer, tm=tm_s, tf=512))

    ret_p, ret_s = [], []
    k_p = v_p = k_s = v_s = kb_p = vb_p = None
    for layer in range(depth):
        if layer == n_a:
            k_p, v_p, kb_p, vb_p = _shared_kv(xp, kv_norm, w_kv, k_norm, tm=tm_big, tn=512)
            k_s, v_s, _, _ = _shared_kv(xs, kv_norm, w_kv, k_norm, tm=tm_s, tn=512)

        xp, xs = ffn_pair(xp, xs, ffn1_norm[layer], 0, layer)

        if layer < n_a:
            a = layer
            w_in, w_out = bf(ret_w_in[a]), bf(ret_w_out[a])
            proj_p = _ret_in(xp, mix_norm[layer], w_in, cos_p, sin_p, tm=tm_big, tn=1024, out_dtype=BF16)
            proj_s = _ret_in(xs, mix_norm[layer], w_in, cos_s, sin_s, tm=tm_s, tn=1024, out_dtype=F32)
            og_p, st_p = _ret_prompt(proj_p, ret_gn[a], batch=batch, seq=seq)
            og_s, st_s = _ret_sample(proj_s.reshape(dec_b, dec_s, -1), state_ret[a], ret_gn[a])
            ret_p.append(st_p)
            ret_s.append(st_s)
            xp = _matmul_res(og_p, w_out, xp, tm=tm_big, tn=1024)
            xs = _matmul_res(og_s.reshape(ms, -1), w_out, xs, tm=tm_s, tn=512)
        else:
            bl = layer - n_a
            w_q, w_out = bf(sb_w_q[bl]), bf(sb_w_out[bl])
            q_p = _sb_query(xp, mix_norm[layer], w_q, sb_q_norm[bl], tm=tm_big, tn=2048, out_dtype=BF16)
            q_s = _sb_query(xs, mix_norm[layer], w_q, sb_q_norm[bl], tm=tm_s, tn=1024, out_dtype=F32)
            at_p = _sb_prompt(q_p, kb_p, vb_p, sb_bias[bl], batch=batch, seq=seq, tq=256, heads=16)
            q_hq = q_s.reshape(dec_b, dec_s, SB_HEADS, SB_HD).transpose(0, 2, 1, 3)
            at_hq = _sb_sample(q_hq.reshape(dec_b, SB_HEADS * dec_s, SB_HD),
                               k_s.reshape(dec_b, dec_s, *grouped), v_s.reshape(dec_b, dec_s, *grouped),
                               cache_k2, cache_v2, page_table, sb_bias[bl], pages_per_step=8)
            at_s = at_hq.reshape(dec_b, SB_HEADS, dec_s, SB_HD).transpose(0, 2, 1, 3).reshape(ms, d)
            xp = _matmul_res(at_p, w_out, xp, tm=tm_big, tn=1024)
            xs = _matmul_res(at_s, w_out, xs, tm=tm_s, tn=1024)

        xp, xs = ffn_pair(xp, xs, ffn2_norm[layer], 1, layer)

    hd_shape = (SB_HEADS, SB_HD)
    return (xp.reshape(batch, seq, d), xs.reshape(dec_b, dec_s, d),
            jnp.stack(ret_p, axis=0), jnp.stack(ret_s, axis=0),
            k_p.reshape(batch, seq, *hd_shape), v_p.reshape(batch, seq, *hd_shape),
            k_s.reshape(dec_b, dec_s, *hd_shape), v_s.reshape(dec_b, dec_s, *hd_shape))
```
